```python
import jax, jax.numpy as jnp
from jax import lax
import numpy as np

D_MODEL = 1024
BATCH = 2
SEQ = 8192
DEPTH = 1

EPS = 1e-6
SSD_HEAD_DIM = 64
SSD_HEADS = 16
SSD_WIDTH = SSD_HEADS * SSD_HEAD_DIM
SSD_GROUPS = 2
D_STATE = 128
D_CONV = 3
CHUNK = 128
HEAD_DIM = 64
N_HEADS = 16
N_KV_HEADS = 4
ATTN_WIDTH = N_HEADS * HEAD_DIM
KV_WIDTH = N_KV_HEADS * HEAD_DIM
ROT_DIM = HEAD_DIM // 4
ROPE_THETA = 500000.0
WINDOW = 128
BLOCK = 128
MIX_WIDTH = SSD_WIDTH + ATTN_WIDTH
PLE_DIM = 256
BC_WIDTH = SSD_GROUPS * D_STATE
XBC_WIDTH = SSD_WIDTH + 2 * BC_WIDTH
IN_SPLIT_SIZES = (SSD_WIDTH, XBC_WIDTH, 2 * SSD_HEADS, ATTN_WIDTH, KV_WIDTH, KV_WIDTH, ATTN_WIDTH)
IN_WIDTH = int(sum(IN_SPLIT_SIZES))

kernel_name = "hybrid_ssd_swa_parallel_heads_encoder"


def _rmsnorm(x, w):
    xf = x.astype(jnp.float32)
    xf = xf * lax.rsqrt(jnp.mean(xf * xf, axis=-1, keepdims=True) + EPS)
    return (xf * w.astype(jnp.float32)).astype(x.dtype)


def _ssd_chunked(x, dt, A, B, C):
    b, l, h, p = x.shape
    g, n = B.shape[2], B.shape[3]
    r = h // g
    c = l // CHUNK
    x = x.reshape(b, c, CHUNK, g, r, p)
    dt = dt.reshape(b, c, CHUNK, g, r)
    B = B.reshape(b, c, CHUNK, g, n)
    C = C.reshape(b, c, CHUNK, g, n)
    xdt = x * dt[..., None]
    a_cum = jnp.cumsum(dt * A.reshape(g, r), axis=2)
    seg = a_cum[:, :, :, None] - a_cum[:, :, None, :]
    tril = jnp.tril(jnp.ones((CHUNK, CHUNK), dtype=bool))[:, :, None, None]
    decay = jnp.exp(jnp.where(tril, seg, -jnp.inf))
    cb = jnp.einsum('bcqgn,bcsgn->bcqsg', C, B)
    y_diag = jnp.einsum('bcqsg,bcqsgr,bcsgrp->bcqgrp', cb, decay, xdt)
    decay_to_end = jnp.exp(a_cum[:, :, -1:] - a_cum)
    states = jnp.einsum('bcsgn,bcsgr,bcsgrp->bcgrpn', B, decay_to_end, xdt)
    chunk_decay = jnp.exp(a_cum[:, :, -1])

    def step(hc, inp):
        dec, st = inp
        return hc * dec[..., None, None] + st, hc

    h0 = jnp.zeros((b, g, r, p, n), jnp.float32)
    _, prev = lax.scan(step, h0, (jnp.moveaxis(chunk_decay, 1, 0), jnp.moveaxis(states, 1, 0)))
    prev = jnp.moveaxis(prev, 0, 1)
    y_off = jnp.einsum('bcqgn,bcgrpn,bcqgr->bcqgrp', C, prev, jnp.exp(a_cum))
    return (y_diag + y_off).reshape(b, l, h, p)


def _rope_partial(t, cos, sin):
    half = ROT_DIM // 2
    tf = t.astype(jnp.float32)
    t1, t2, rest = tf[..., :half], tf[..., half:ROT_DIM], tf[..., ROT_DIM:]
    c = cos[None, :, None, :]
    s = sin[None, :, None, :]
    out = jnp.concatenate([t1 * c - t2 * s, t2 * c + t1 * s, rest], axis=-1)
    return out.astype(t.dtype)


def _window_attention(q, k, v, sink):
    b, l, h, d = q.shape
    kvh = k.shape[2]
    r = h // kvh
    nb = l // BLOCK
    qb = q.reshape(b, nb, BLOCK, kvh, r, d)
    pad = ((0, 0), (BLOCK, BLOCK), (0, 0), (0, 0))
    kp = jnp.pad(k, pad).reshape(b, nb + 2, BLOCK, kvh, d)
    vp = jnp.pad(v, pad).reshape(b, nb + 2, BLOCK, kvh, d)
    kw = jnp.concatenate([kp[:, :-2], kp[:, 1:-1], kp[:, 2:]], axis=2)
    vw = jnp.concatenate([vp[:, :-2], vp[:, 1:-1], vp[:, 2:]], axis=2)
    scale = HEAD_DIM ** -0.5
    s = jnp.einsum('bnqkrd,bnskd->bnkrqs', qb, kw).astype(jnp.float32) * scale
    blk = jnp.arange(nb)[:, None] * BLOCK
    qpos = blk + jnp.arange(BLOCK)[None, :]
    kpos = blk - BLOCK + jnp.arange(3 * BLOCK)[None, :]
    valid = (jnp.abs(qpos[:, :, None] - kpos[:, None, :]) <= WINDOW) \
        & (kpos >= 0)[:, None, :] & (kpos < l)[:, None, :]
    s = jnp.where(valid[None, :, None, None], s, -jnp.inf)
    sink_l = sink.astype(jnp.float32).reshape(kvh, r)[None, None, :, :, None, None]
    m = jnp.maximum(jnp.max(s, axis=-1, keepdims=True), sink_l)
    e = jnp.exp(s - m)
    probs = e / (jnp.sum(e, axis=-1, keepdims=True) + jnp.exp(sink_l - m))
    o = jnp.einsum('bnkrqs,bnskd->bnqkrd', probs.astype(v.dtype), vw)
    return o.reshape(b, l, h, d)


def setup_inputs(seed: int = 0) -> dict:
    key = jax.random.key(seed)
    ks = jax.random.split(key, 20)
    f32 = jnp.float32
    x = jax.random.normal(ks[0], (BATCH, SEQ, D_MODEL), f32)
    p = jax.random.normal(ks[1], (DEPTH, BATCH, SEQ, PLE_DIM), f32)
    norm_w = 1.0 + 0.02 * jax.random.normal(ks[2], (DEPTH, D_MODEL), f32)
    w_in = jax.random.normal(ks[3], (DEPTH, D_MODEL, IN_WIDTH), f32) * D_MODEL ** -0.5
    conv_w = jax.random.normal(ks[4], (DEPTH, D_CONV, XBC_WIDTH), f32) * D_CONV ** -0.5
    conv_b = 0.02 * jax.random.normal(ks[5], (DEPTH, XBC_WIDTH), f32)
    dt_f = jnp.exp(jax.random.uniform(ks[6], (DEPTH, SSD_HEADS), f32, np.log(1e-3), np.log(1e-1)))
    dt_b = jnp.exp(jax.random.uniform(ks[7], (DEPTH, SSD_HEADS), f32, np.log(1e-3), np.log(1e-1)))
    dt_bias_f = dt_f + jnp.log(-jnp.expm1(-dt_f))
    dt_bias_b = dt_b + jnp.log(-jnp.expm1(-dt_b))
    a_log_f = jnp.log(jax.random.uniform(ks[8], (DEPTH, SSD_HEADS), f32, 1.0, 16.0))
    a_log_b = jnp.log(jax.random.uniform(ks[9], (DEPTH, SSD_HEADS), f32, 1.0, 16.0))
    d_skip = 1.0 + 0.1 * jax.random.normal(ks[10], (DEPTH, SSD_HEADS), f32)
    ssd_norm_w = 1.0 + 0.02 * jax.random.normal(ks[11], (DEPTH, SSD_WIDTH), f32)
    attn_sink = 0.5 * jax.random.normal(ks[12], (DEPTH, N_HEADS), f32)
    attn_norm_w = 1.0 + 0.02 * jax.random.normal(ks[13], (DEPTH, ATTN_WIDTH), f32)
    w_out = jax.random.normal(ks[14], (DEPTH, MIX_WIDTH, D_MODEL), f32) * MIX_WIDTH ** -0.5
    ple_proj = jax.random.normal(ks[15], (DEPTH, PLE_DIM, D_MODEL), f32) * PLE_DIM ** -0.5
    ple_gate_w = jax.random.normal(ks[16], (DEPTH, D_MODEL, D_MODEL), f32) * D_MODEL ** -0.5
    ple_gate_b = 0.02 * jax.random.normal(ks[17], (DEPTH, D_MODEL), f32)
    final_norm_w = 1.0 + 0.02 * jax.random.normal(ks[18], (D_MODEL,), f32)
    return {"x": x, "p": p, "norm_w": norm_w, "w_in": w_in, "conv_w": conv_w, "conv_b": conv_b,
            "dt_bias_f": dt_bias_f, "dt_bias_b": dt_bias_b, "a_log_f": a_log_f, "a_log_b": a_log_b,
            "d_skip": d_skip, "ssd_norm_w": ssd_norm_w, "attn_sink": attn_sink,
            "attn_norm_w": attn_norm_w, "w_out": w_out, "ple_proj": ple_proj,
            "ple_gate_w": ple_gate_w, "ple_gate_b": ple_gate_b, "final_norm_w": final_norm_w}


def reference(x, p, norm_w, w_in, conv_w, conv_b, dt_bias_f, dt_bias_b, a_log_f, a_log_b,
              d_skip, ssd_norm_w, attn_sink, attn_norm_w, w_out, ple_proj, ple_gate_w,
              ple_gate_b, final_norm_w):
    b, l, _ = x.shape
    f32 = jnp.float32
    inv_freq = ROPE_THETA ** (-jnp.arange(0, ROT_DIM, 2, dtype=f32) / ROT_DIM)
    ang = jnp.arange(l, dtype=f32)[:, None] * inv_freq[None, :]
    cos, sin = jnp.cos(ang), jnp.sin(ang)
    split_idx = [int(v) for v in np.cumsum(IN_SPLIT_SIZES)[:-1]]

    for i in range(DEPTH):
        h = _rmsnorm(x, norm_w[i])
        proj = h @ w_in[i]
        z, xbc, dt_raw, q, k, v, g = jnp.split(proj, split_idx, axis=-1)

        xbc = lax.conv_general_dilated(
            xbc, conv_w[i][:, None, :].astype(xbc.dtype), window_strides=(1,),
            padding=[((D_CONV - 1) // 2, (D_CONV - 1) // 2)],
            dimension_numbers=('NWC', 'WIO', 'NWC'), feature_group_count=XBC_WIDTH)
        xbc = jax.nn.silu(xbc + conv_b[i])
        xs, Bm, Cm = jnp.split(xbc, [SSD_WIDTH, SSD_WIDTH + BC_WIDTH], axis=-1)
        xs = xs.reshape(b, l, SSD_HEADS, SSD_HEAD_DIM).astype(f32)
        Bm = Bm.reshape(b, l, SSD_GROUPS, D_STATE).astype(f32)
        Cm = Cm.reshape(b, l, SSD_GROUPS, D_STATE).astype(f32)
        dt_raw = dt_raw.astype(f32)
        dt_fw = jax.nn.softplus(dt_raw[..., :SSD_HEADS] + dt_bias_f[i].astype(f32))
        dt_bw = jax.nn.softplus(dt_raw[..., SSD_HEADS:] + dt_bias_b[i].astype(f32))
        A_fw = -jnp.exp(a_log_f[i].astype(f32))
        A_bw = -jnp.exp(a_log_b[i].astype(f32))
        y_fw = _ssd_chunked(xs, dt_fw, A_fw, Bm, Cm)
        y_bw = jnp.flip(_ssd_chunked(jnp.flip(xs, 1), jnp.flip(dt_bw, 1), A_bw,
                                     jnp.flip(Bm, 1), jnp.flip(Cm, 1)), 1)
        y = y_fw + y_bw + d_skip[i].astype(f32)[:, None] * xs
        y = y.reshape(b, l, SSD_WIDTH)
        y_ssd = _rmsnorm(y * jax.nn.silu(z.astype(f32)), ssd_norm_w[i]).astype(x.dtype)

        q = _rope_partial(q.reshape(b, l, N_HEADS, HEAD_DIM), cos, sin)
        k = _rope_partial(k.reshape(b, l, N_KV_HEADS, HEAD_DIM), cos, sin)
        v = v.reshape(b, l, N_KV_HEADS, HEAD_DIM)
        o = _window_attention(q, k, v, attn_sink[i]).reshape(b, l, ATTN_WIDTH)
        y_attn = _rmsnorm(o * jax.nn.silu(g), attn_norm_w[i])

        x = x + jnp.concatenate([y_ssd, y_attn], axis=-1) @ w_out[i]

        gate = jax.nn.sigmoid((x @ ple_gate_w[i] + ple_gate_b[i]).astype(f32))
        x = x + (gate * (p[i] @ ple_proj[i]).astype(f32)).astype(x.dtype)

    return _rmsnorm(x, final_norm_w)
```

```python
import functools

import numpy as np
import jax
import jax.numpy as jnp
from jax import lax
from jax.experimental import pallas as pl
from jax.experimental.pallas import tpu as pltpu

F32 = jnp.float32
BF16 = jnp.bfloat16

EPS = 1e-6
D_MODEL = 1024
SSD_HEADS = 16
SSD_HEAD_DIM = 64
SSD_WIDTH = SSD_HEADS * SSD_HEAD_DIM
SSD_GROUPS = 2
D_STATE = 128
BC_WIDTH = SSD_GROUPS * D_STATE
XBC_WIDTH = SSD_WIDTH + 2 * BC_WIDTH
CHUNK = 128
HEAD_DIM = 64
N_HEADS = 16
N_KV_HEADS = 4
ATTN_WIDTH = N_HEADS * HEAD_DIM
KV_WIDTH = N_KV_HEADS * HEAD_DIM
ROT_DIM = HEAD_DIM // 4
ROPE_THETA = 500000.0
BLOCK = 128
PLE_DIM = 256

LANES = 128
HALO_ROWS = 16
VMEM_LIMIT = 56 * 1024 * 1024

HEAD_PERM = (0, 4, 1, 5, 2, 6, 3, 7, 8, 12, 9, 13, 10, 14, 11, 15)

Z0, Z1 = 0, SSD_WIDTH
X0, X1 = Z1, Z1 + XBC_WIDTH
Q0, Q1 = X1, X1 + ATTN_WIDTH
K0, K1 = Q1, Q1 + KV_WIDTH
V0, V1 = K1, K1 + KV_WIDTH
G0, G1 = V1, V1 + ATTN_WIDTH


def _dot(a, b, precision=None):
    return jnp.dot(a, b, preferred_element_type=F32, precision=precision)


def _dot_nt(a, b):
    return lax.dot_general(a, b, (((1,), (1,)), ((), ())), preferred_element_type=F32)


def _const_spec(shape, single_buffer=False):
    nd = len(shape)
    kw = {"pipeline_mode": pl.Buffered(1)} if single_buffer else {}
    return pl.BlockSpec(shape, lambda *_: (0,) * nd, **kw)


def _silu(x):
    return x / (1.0 + jnp.exp(-x))


def _in_proj_kernel(x_ref, nw_ref, w_ref, wdt_ref, cos_ref, sina_ref, sinb_ref,
                    z_ref, xbc_ref, dtf_ref, dtb_ref, q_ref, k_ref, v_ref, g_ref):
    x = x_ref[...]
    ms = jnp.mean(x * x, axis=-1, keepdims=True)
    h = (x * lax.rsqrt(ms + EPS) * nw_ref[...]).astype(BF16)

    z_ref[...] = _dot(h, w_ref[:, Z0:Z1]).astype(BF16)
    xbc_ref[...] = _dot(h, w_ref[:, X0:X1]).astype(BF16)
    v_ref[...] = _dot(h, w_ref[:, V0:V1]).astype(BF16)
    g_ref[...] = _dot(h, w_ref[:, G0:G1]).astype(BF16)
    dt = _dot(h, wdt_ref[...])
    dtf_ref[...] = dt[:, :LANES]
    dtb_ref[...] = dt[:, LANES:]

    cosf = cos_ref[...]
    sina = sina_ref[...]
    sinb = sinb_ref[...]

    def rope(t):
        return (t * cosf + pltpu.roll(t, LANES - ROT_DIM // 2, 1) * sina
                + pltpu.roll(t, ROT_DIM // 2, 1) * sinb)

    q = _dot(h, w_ref[:, Q0:Q1]) * (HEAD_DIM ** -0.5)
    for j in range(ATTN_WIDTH // LANES):
        q_ref[:, j * LANES:(j + 1) * LANES] = rope(q[:, j * LANES:(j + 1) * LANES]).astype(BF16)
    k = _dot(h, w_ref[:, K0:K1])
    for j in range(KV_WIDTH // LANES):
        k_ref[:, j * LANES:(j + 1) * LANES] = rope(k[:, j * LANES:(j + 1) * LANES]).astype(BF16)


def _in_proj(x2, norm_w, w_main, w_dt, cosf, sina, sinb, seq, tm):
    t = x2.shape[0]
    nblk_seq = seq // tm
    row = lambda i: (i, 0)
    tab = lambda i: (i % nblk_seq, 0)
    out_shapes = (
        jax.ShapeDtypeStruct((t, SSD_WIDTH), BF16),
        jax.ShapeDtypeStruct((t, XBC_WIDTH), BF16),
        jax.ShapeDtypeStruct((t, LANES), F32),
        jax.ShapeDtypeStruct((t, LANES), F32),
        jax.ShapeDtypeStruct((t, ATTN_WIDTH), BF16),
        jax.ShapeDtypeStruct((t, KV_WIDTH), BF16),
        jax.ShapeDtypeStruct((t, KV_WIDTH), BF16),
        jax.ShapeDtypeStruct((t, ATTN_WIDTH), BF16),
    )
    return pl.pallas_call(
        _in_proj_kernel,
        grid=(t // tm,),
        in_specs=[
            pl.BlockSpec((tm, D_MODEL), row),
            _const_spec((1, D_MODEL)),
            _const_spec(w_main.shape, single_buffer=True),
            _const_spec(w_dt.shape, single_buffer=True),
            pl.BlockSpec((tm, LANES), tab),
            pl.BlockSpec((tm, LANES), tab),
            pl.BlockSpec((tm, LANES), tab),
        ],
        out_specs=tuple(pl.BlockSpec((tm, s.shape[1]), row) for s in out_shapes),
        out_shape=out_shapes,
        compiler_params=pltpu.CompilerParams(
            dimension_semantics=("arbitrary",), vmem_limit_bytes=VMEM_LIMIT),
        name="in_proj",
    )(x2, norm_w, w_main, w_dt, cosf, sina, sinb)


def _ssd_direction(pre_ref, prev_ref, next_ref, dt_ref, bias, a_neg, cw, cb, e_bf, e_f32,
                   h_ref, prev_valid, next_valid, reverse):
    pre = pre_ref[...].astype(F32)
    prow = jnp.where(prev_valid, prev_ref[...].astype(F32)[HALO_ROWS - 1:HALO_ROWS, :], 0.0)
    nrow = jnp.where(next_valid, next_ref[...].astype(F32)[0:1, :], 0.0)
    rows = lax.broadcasted_iota(jnp.int32, (CHUNK, 1), 0)
    up = jnp.where(rows == 0, prow, pltpu.roll(pre, 1, 0))
    dn = jnp.where(rows == CHUNK - 1, nrow, pltpu.roll(pre, CHUNK - 1, 0))
    acc = up * cw[0:1, :] + pre * cw[1:2, :] + dn * cw[2:3, :] + cb
    xc = _silu(acc)
    xs = xc[:, :SSD_WIDTH]
    bm = xc[:, SSD_WIDTH:SSD_WIDTH + BC_WIDTH]
    cm = xc[:, SSD_WIDTH + BC_WIDTH:]
    xs_b = xs.astype(BF16)
    bm_b = bm.astype(BF16)
    cm_b = cm.astype(BF16)

    dtr = dt_ref[...] + bias
    dt = jnp.maximum(dtr, 0.0) + jnp.log1p(jnp.exp(-jnp.abs(dtr)))
    a = dt * a_neg
    ri = lax.broadcasted_iota(jnp.int32, (CHUNK, CHUNK), 0)
    ci = lax.broadcasted_iota(jnp.int32, (CHUNK, CHUNK), 1)
    tri = (ci >= ri) if reverse else (ci <= ri)
    acum = _dot(tri.astype(F32), a, precision=lax.Precision.HIGHEST)
    acum_t = acum.T
    dt_t = dt.T
    last = 0 if reverse else CHUNK - 1
    a_last = acum[last:last + 1, :]
    dte = jnp.exp(a_last - acum)
    sc_exp = _dot((dte * dt).astype(BF16), e_bf)
    xsc = (xs * sc_exp).astype(BF16)
    ecol = jnp.exp(acum)
    a_last_exp = _dot(jnp.broadcast_to(a_last, (8, LANES)), e_f32,
                      precision=lax.Precision.HIGHEST)[0:1, :]
    cd_exp = jnp.exp(a_last_exp)

    gmat = [_dot_nt(cm_b[:, g * D_STATE:(g + 1) * D_STATE], bm_b[:, g * D_STATE:(g + 1) * D_STATE])
            for g in range(SSD_GROUPS)]
    hprev = [h_ref[g] for g in range(SSD_GROUPS)]
    hprev_b = [h.astype(BF16) for h in hprev]
    low = lax.broadcasted_iota(jnp.int32, (CHUNK, LANES), 1) < SSD_HEAD_DIM
    low2 = lax.broadcasted_iota(jnp.int32, (D_STATE, LANES), 1) < SSD_HEAD_DIM
    heads_per_group = SSD_HEADS // SSD_GROUPS
    pairs_per_group = heads_per_group // 2
    outs = []
    for j in range(SSD_HEADS // 2):
        g = j // pairs_per_group
        jj = j % pairs_per_group
        xs_pair = xs_b[:, j * LANES:(j + 1) * LANES]
        h_pair = hprev_b[g][:, jj * LANES:(jj + 1) * LANES]
        cg = cm[:, g * D_STATE:(g + 1) * D_STATE]
        acc_y = None
        for e in range(2):
            hd = 2 * j + e
            seg = acum[:, hd:hd + 1] - acum_t[hd:hd + 1, :]
            dec = jnp.exp(jnp.where(tri, seg, -jnp.inf))
            m = (gmat[g] * dec * dt_t[hd:hd + 1, :]).astype(BF16)
            cs = (cg * ecol[:, hd:hd + 1]).astype(BF16)
            lhs = jnp.concatenate([m, cs], axis=1)
            sel = low if e == 0 else jnp.logical_not(low)
            sel2 = low2 if e == 0 else jnp.logical_not(low2)
            rhs = jnp.concatenate([jnp.where(sel, xs_pair, jnp.zeros_like(xs_pair)),
                                   jnp.where(sel2, h_pair, jnp.zeros_like(h_pair))], axis=0)
            r = _dot(lhs, rhs)
            acc_y = r if acc_y is None else acc_y + r
        outs.append(acc_y)
    y = jnp.concatenate(outs, axis=1)

    gw = heads_per_group * SSD_HEAD_DIM
    for g in range(SSD_GROUPS):
        bt = bm[:, g * D_STATE:(g + 1) * D_STATE].T.astype(BF16)
        s_new = _dot(bt, xsc[:, g * gw:(g + 1) * gw])
        h_ref[g] = hprev[g] * cd_exp[:, g * gw:(g + 1) * gw] + s_new
    return y, xs


def _ssd_kernel(xf_ref, xfp_ref, xfn_ref, xb_ref, xbp_ref, xbn_ref, dtf_ref, dtb_ref,
                cw_ref, cb_ref, bias_ref, alog_ref, dskip_ref, ebf_ref, ef32_ref,
                yf_ref, yb_ref, hf_ref, hb_ref):
    c = pl.program_id(1)
    nc = pl.num_programs(1)

    @pl.when(c == 0)
    def _():
        hf_ref[...] = jnp.zeros_like(hf_ref)
        hb_ref[...] = jnp.zeros_like(hb_ref)

    cw = cw_ref[...]
    cb = cb_ref[...]
    e_bf = ebf_ref[...]
    e_f32 = ef32_ref[...]
    head_lane = lax.broadcasted_iota(jnp.int32, (1, LANES), 1) < SSD_HEADS
    a_neg = jnp.where(head_lane, -jnp.exp(alog_ref[...]), 0.0)
    bias = bias_ref[...]

    yf, xs_f = _ssd_direction(xf_ref, xfp_ref, xfn_ref, dtf_ref, bias[0:1], a_neg[0:1], cw, cb,
                              e_bf, e_f32, hf_ref, c > 0, c < nc - 1, reverse=False)
    yf_ref[...] = (yf + dskip_ref[...] * xs_f).astype(BF16)
    yb, _ = _ssd_direction(xb_ref, xbp_ref, xbn_ref, dtb_ref, bias[1:2], a_neg[1:2], cw, cb,
                           e_bf, e_f32, hb_ref, c < nc - 1, c > 0, reverse=True)
    yb_ref[...] = yb.astype(BF16)


def _ssd(xbc, dtf, dtb, conv_w, conv_b, bias2, alog2, dskip_exp, e_bf, e_f32, batch, seq):
    t = xbc.shape[0]
    nc = seq // CHUNK
    hpc = CHUNK // HALO_ROWS
    nhalo = t // HALO_ROWS
    fw = lambda b, c: (b * nc + c, 0)
    bw = lambda b, c: (b * nc + nc - 1 - c, 0)
    fw_prev = lambda b, c: (jnp.maximum((b * nc + c) * hpc - 1, 0), 0)
    fw_next = lambda b, c: (jnp.minimum((b * nc + c + 1) * hpc, nhalo - 1), 0)
    bw_prev = lambda b, c: (jnp.maximum((b * nc + nc - 1 - c) * hpc - 1, 0), 0)
    bw_next = lambda b, c: (jnp.minimum((b * nc + nc - c) * hpc, nhalo - 1), 0)
    gw = (SSD_HEADS // SSD_GROUPS) * SSD_HEAD_DIM
    return pl.pallas_call(
        _ssd_kernel,
        grid=(batch, nc),
        in_specs=[
            pl.BlockSpec((CHUNK, XBC_WIDTH), fw),
            pl.BlockSpec((HALO_ROWS, XBC_WIDTH), fw_prev),
            pl.BlockSpec((HALO_ROWS, XBC_WIDTH), fw_next),
            pl.BlockSpec((CHUNK, XBC_WIDTH), bw),
            pl.BlockSpec((HALO_ROWS, XBC_WIDTH), bw_prev),
            pl.BlockSpec((HALO_ROWS, XBC_WIDTH), bw_next),
            pl.BlockSpec((CHUNK, LANES), fw),
            pl.BlockSpec((CHUNK, LANES), bw),
            _const_spec(conv_w.shape),
            _const_spec(conv_b.shape),
            _const_spec(bias2.shape),
            _const_spec(alog2.shape),
            _const_spec(dskip_exp.shape),
            _const_spec(e_bf.shape),
            _const_spec(e_f32.shape),
        ],
        out_specs=(pl.BlockSpec((CHUNK, SSD_WIDTH), fw), pl.BlockSpec((CHUNK, SSD_WIDTH), bw)),
        out_shape=(jax.ShapeDtypeStruct((t, SSD_WIDTH), BF16),
                   jax.ShapeDtypeStruct((t, SSD_WIDTH), BF16)),
        scratch_shapes=[pltpu.VMEM((SSD_GROUPS, D_STATE, gw), F32),
                        pltpu.VMEM((SSD_GROUPS, D_STATE, gw), F32)],
        compiler_params=pltpu.CompilerParams(
            dimension_semantics=("arbitrary", "arbitrary"), vmem_limit_bytes=VMEM_LIMIT),
        name="ssd",
    )(xbc, xbc, xbc, xbc, xbc, xbc, dtf, dtb, conv_w, conv_b, bias2, alog2, dskip_exp, e_bf, e_f32)


def _attn_kernel(sink_ref, q_ref, kp_ref, kc_ref, kn_ref, vp_ref, vc_ref, vn_ref, g_ref, nw_ref, o_ref):
    nb = pl.program_id(1)
    nblk = pl.num_programs(1)
    q = q_ref[...]
    kcat = jnp.concatenate([kp_ref[...], kc_ref[...], kn_ref[...]], axis=0)
    vcat = jnp.concatenate([vp_ref[...], vc_ref[...], vn_ref[...]], axis=0)

    r = lax.broadcasted_iota(jnp.int32, (BLOCK, 3 * BLOCK), 0)
    cc = lax.broadcasted_iota(jnp.int32, (BLOCK, 3 * BLOCK), 1)
    lo_lim = jnp.where(nb > 0, 0, BLOCK)
    hi_lim = jnp.where(nb < nblk - 1, 3 * BLOCK, 2 * BLOCK)
    valid = (cc >= r) & (cc <= r + 2 * BLOCK) & (cc >= lo_lim) & (cc < hi_lim)
    bias = jnp.where(valid, 0.0, -jnp.inf).astype(F32)

    low = lax.broadcasted_iota(jnp.int32, (BLOCK, LANES), 1) < HEAD_DIM
    low3 = lax.broadcasted_iota(jnp.int32, (3 * BLOCK, LANES), 1) < HEAD_DIM
    zq = jnp.zeros((BLOCK, LANES), BF16)
    tiles_per_pair = (N_HEADS // N_KV_HEADS)
    outs = []
    for j in range(N_KV_HEADS // 2):
        k_tile = kcat[:, j * LANES:(j + 1) * LANES]
        v_tile = vcat[:, j * LANES:(j + 1) * LANES]
        v_lo = jnp.where(low3, v_tile, jnp.zeros_like(v_tile))
        v_hi = jnp.where(low3, jnp.zeros_like(v_tile), v_tile)
        v2 = jnp.concatenate([v_lo, v_hi], axis=0)
        lhs_parts = []
        for tt in range(tiles_per_pair):
            qt = q[:, (j * tiles_per_pair + tt) * LANES:(j * tiles_per_pair + tt + 1) * LANES]
            lhs_parts.append(jnp.where(low, qt, zq))
            lhs_parts.append(jnp.where(low, zq, qt))
        s_all = _dot_nt(jnp.concatenate(lhs_parts, axis=0), k_tile)
        for tt in range(tiles_per_pair):
            ps = []
            invs = []
            for e in range(2):
                idx = 2 * tt + e
                head = HEAD_PERM[2 * (j * tiles_per_pair + tt) + e]
                sink = sink_ref[head]
                s = s_all[idx * BLOCK:(idx + 1) * BLOCK, :] + bias
                m = jnp.maximum(jnp.max(s, axis=-1, keepdims=True), sink)
                p = jnp.exp(s - m)
                den = jnp.sum(p, axis=-1, keepdims=True) + jnp.exp(sink - m)
                ps.append(p.astype(BF16))
                invs.append(1.0 / den)
            o_t = _dot(jnp.concatenate(ps, axis=1), v2)
            outs.append(o_t * jnp.where(low, invs[0], invs[1]))
    o = jnp.concatenate(outs, axis=1)
    y = o * _silu(g_ref[...].astype(F32))
    ms = jnp.mean(y * y, axis=-1, keepdims=True)
    o_ref[...] = (y * lax.rsqrt(ms + EPS) * nw_ref[...]).astype(BF16)


def _attn(sink, q, k, v, g, norm_w, batch, seq):
    t = q.shape[0]
    nblk = seq // BLOCK
    cur = lambda b, n: (b * nblk + n, 0)
    prev = lambda b, n: (b * nblk + jnp.maximum(n - 1, 0), 0)
    nxt = lambda b, n: (b * nblk + jnp.minimum(n + 1, nblk - 1), 0)
    return pl.pallas_call(
        _attn_kernel,
        grid=(batch, nblk),
        in_specs=[
            pl.BlockSpec(memory_space=pltpu.SMEM),
            pl.BlockSpec((BLOCK, ATTN_WIDTH), cur),
            pl.BlockSpec((BLOCK, KV_WIDTH), prev),
            pl.BlockSpec((BLOCK, KV_WIDTH), cur),
            pl.BlockSpec((BLOCK, KV_WIDTH), nxt),
            pl.BlockSpec((BLOCK, KV_WIDTH), prev),
            pl.BlockSpec((BLOCK, KV_WIDTH), cur),
            pl.BlockSpec((BLOCK, KV_WIDTH), nxt),
            pl.BlockSpec((BLOCK, ATTN_WIDTH), cur),
            _const_spec(norm_w.shape),
        ],
        out_specs=pl.BlockSpec((BLOCK, ATTN_WIDTH), cur),
        out_shape=jax.ShapeDtypeStruct((t, ATTN_WIDTH), BF16),
        compiler_params=pltpu.CompilerParams(
            dimension_semantics=("arbitrary", "arbitrary"), vmem_limit_bytes=VMEM_LIMIT),
        name="attn",
    )(sink, q, k, k, k, v, v, v, g, norm_w)


def _out_kernel(x_ref, yf_ref, yb_ref, z_ref, ya_ref, p_ref, snw_ref, wo_ref, wg_ref, bg_ref, wp_ref,
                fnw_ref, o_ref):
    y = yf_ref[...].astype(F32) + yb_ref[...].astype(F32)
    y = y * _silu(z_ref[...].astype(F32))
    ms = jnp.mean(y * y, axis=-1, keepdims=True)
    y_ssd = (y * lax.rsqrt(ms + EPS) * snw_ref[...]).astype(BF16)
    x1 = (x_ref[...] + _dot(y_ssd, wo_ref[:SSD_WIDTH, :]) + _dot(ya_ref[...], wo_ref[SSD_WIDTH:, :]))
    gate_lin = _dot(x1.astype(BF16), wg_ref[...]) + bg_ref[...]
    gate = 1.0 / (1.0 + jnp.exp(-gate_lin))
    x2 = x1 + gate * _dot(p_ref[...].astype(BF16), wp_ref[...])
    ms2 = jnp.mean(x2 * x2, axis=-1, keepdims=True)
    o_ref[...] = x2 * lax.rsqrt(ms2 + EPS) * fnw_ref[...]


def _out(x2, yf, yb, z, ya, p2, ssd_norm_w, w_out, w_gate, b_gate, w_ple, final_norm_w, tm):
    t = x2.shape[0]
    row = lambda i: (i, 0)
    return pl.pallas_call(
        _out_kernel,
        grid=(t // tm,),
        in_specs=[
            pl.BlockSpec((tm, D_MODEL), row),
            pl.BlockSpec((tm, SSD_WIDTH), row),
            pl.BlockSpec((tm, SSD_WIDTH), row),
            pl.BlockSpec((tm, SSD_WIDTH), row),
            pl.BlockSpec((tm, ATTN_WIDTH), row),
            pl.BlockSpec((tm, PLE_DIM), row),
            _const_spec(ssd_norm_w.shape),
            _const_spec(w_out.shape, single_buffer=True),
            _const_spec(w_gate.shape, single_buffer=True),
            _const_spec(b_gate.shape),
            _const_spec(w_ple.shape, single_buffer=True),
            _const_spec(final_norm_w.shape),
        ],
        out_specs=pl.BlockSpec((tm, D_MODEL), row),
        out_shape=jax.ShapeDtypeStruct((t, D_MODEL), F32),
        compiler_params=pltpu.CompilerParams(
            dimension_semantics=("arbitrary",), vmem_limit_bytes=VMEM_LIMIT),
        name="out_stage",
    )(x2, yf, yb, z, ya, p2, ssd_norm_w, w_out, w_gate, b_gate, w_ple, final_norm_w)


def _rope_tables(seq):
    inv_freq = ROPE_THETA ** (-jnp.arange(0, ROT_DIM, 2, dtype=F32) / ROT_DIM)
    ang = jnp.arange(seq, dtype=F32)[:, None] * inv_freq[None, :]
    cos, sin = jnp.cos(ang), jnp.sin(ang)
    half = ROT_DIM // 2
    lane = np.arange(LANES) % HEAD_DIM
    idx = np.where(lane < half, lane, np.where(lane < ROT_DIM, lane - half, 0))
    first = jnp.asarray(lane < half)
    second = jnp.asarray((lane >= half) & (lane < ROT_DIM))
    cos_l = cos[:, idx]
    sin_l = sin[:, idx]
    cosf = jnp.where(first | second, cos_l, 1.0)
    sina = jnp.where(first, -sin_l, 0.0)
    sinb = jnp.where(second, sin_l, 0.0)
    return cosf, sina, sinb


def _perm_columns(w, perm):
    parts = [w[..., h * HEAD_DIM:(h + 1) * HEAD_DIM] for h in perm]
    return jnp.concatenate(parts, axis=-1)


def _layer(x2, p2, norm_w, w_in, conv_w, conv_b, dt_bias_f, dt_bias_b, a_log_f, a_log_b, d_skip,
           ssd_norm_w, attn_sink, attn_norm_w, w_out, ple_proj, ple_gate_w, ple_gate_b, tables,
           batch, seq, tm_in, tm_out, e_bf, e_f32):
    split = np.cumsum((SSD_WIDTH, XBC_WIDTH, 2 * SSD_HEADS, ATTN_WIDTH, KV_WIDTH, KV_WIDTH, ATTN_WIDTH))
    w_z = w_in[:, :split[0]]
    w_x = w_in[:, split[0]:split[1]]
    w_dt = w_in[:, split[1]:split[2]]
    w_q = _perm_columns(w_in[:, split[2]:split[3]], HEAD_PERM)
    w_k = w_in[:, split[3]:split[4]]
    w_v = w_in[:, split[4]:split[5]]
    w_g = _perm_columns(w_in[:, split[5]:split[6]], HEAD_PERM)
    w_main = jnp.concatenate([w_z, w_x, w_q, w_k, w_v, w_g], axis=1).astype(BF16)
    pad = jnp.zeros((D_MODEL, LANES - SSD_HEADS), w_in.dtype)
    w_dt2 = jnp.concatenate([w_dt[:, :SSD_HEADS], pad, w_dt[:, SSD_HEADS:], pad], axis=1).astype(BF16)

    z, xbc, dtf, dtb, q, k, v, g = _in_proj(x2, norm_w[None, :], w_main, w_dt2, *tables, seq, tm_in)

    lane_pad = jnp.zeros((LANES - SSD_HEADS,), F32)
    bias2 = jnp.stack([jnp.concatenate([dt_bias_f, lane_pad]), jnp.concatenate([dt_bias_b, lane_pad])])
    alog2 = jnp.stack([jnp.concatenate([a_log_f, lane_pad]), jnp.concatenate([a_log_b, lane_pad])])
    dskip_exp = jnp.repeat(d_skip, SSD_HEAD_DIM)[None, :]
    yf, yb = _ssd(xbc, dtf, dtb, conv_w, conv_b[None, :], bias2, alog2, dskip_exp, e_bf, e_f32, batch, seq)

    ya = _attn(attn_sink, q, k, v, g, _perm_columns(attn_norm_w, HEAD_PERM)[None, :], batch, seq)

    w_out_attn = jnp.concatenate(
        [w_out[SSD_WIDTH + h * HEAD_DIM:SSD_WIDTH + (h + 1) * HEAD_DIM, :] for h in HEAD_PERM], axis=0)
    w_out_b = jnp.concatenate([w_out[:SSD_WIDTH], w_out_attn], axis=0).astype(BF16)
    return (yf, yb, z, ya, w_out_b)


def kernel(x, p, norm_w, w_in, conv_w, conv_b, dt_bias_f, dt_bias_b, a_log_f, a_log_b, d_skip, ssd_norm_w,
           attn_sink, attn_norm_w, w_out, ple_proj, ple_gate_w, ple_gate_b, final_norm_w):
    batch, seq, _ = x.shape
    depth = p.shape[0]
    t = batch * seq
    tm_in = min(512, seq)
    tm_out = min(256, seq)
    tables = _rope_tables(seq)
    head_of_lane = np.arange(SSD_WIDTH) // SSD_HEAD_DIM
    e_np = (np.arange(LANES)[:, None] == head_of_lane[None, :]).astype(np.float32)
    e_f32 = jnp.asarray(e_np)
    e_bf = e_f32.astype(BF16)

    x2 = x.reshape(t, D_MODEL)
    ones = jnp.ones((1, D_MODEL), F32)
    for i in range(depth):
        yf, yb, z, ya, w_out_b = _layer(
            x2, p[i].reshape(t, PLE_DIM), norm_w[i], w_in[i], conv_w[i], conv_b[i], dt_bias_f[i],
            dt_bias_b[i], a_log_f[i], a_log_b[i], d_skip[i], ssd_norm_w[i], attn_sink[i], attn_norm_w[i],
            w_out[i], ple_proj[i], ple_gate_w[i], ple_gate_b[i], tables, batch, seq, tm_in, tm_out,
            e_bf, e_f32)
        last = i == depth - 1
        assert last, "only DEPTH == 1 is supported"
        x2 = _out(x2, yf, yb, z, ya, p[i].reshape(t, PLE_DIM), ssd_norm_w[i][None, :], w_out_b,
                  ple_gate_w[i].astype(BF16), ple_gate_b[i][None, :], ple_proj[i].astype(BF16),
                  final_norm_w[None, :] if last else ones, tm_out)
    return x2.reshape(batch, seq, D_MODEL)
```

```python
import functools
import math

import numpy as np
import jax
import jax.numpy as jnp
from jax import lax
from jax.experimental import pallas as pl
from jax.experimental.pallas import tpu as pltpu

F32 = jnp.float32
BF16 = jnp.bfloat16

EPS = 1e-6
D_MODEL = 1024
SSD_HEADS = 16
SSD_HEAD_DIM = 64
SSD_WIDTH = SSD_HEADS * SSD_HEAD_DIM
SSD_GROUPS = 2
D_STATE = 128
BC_WIDTH = SSD_GROUPS * D_STATE
XBC_WIDTH = SSD_WIDTH + 2 * BC_WIDTH
D_CONV = 3
CHUNK = 128
HEAD_DIM = 64
N_HEADS = 16
N_KV_HEADS = 4
ATTN_WIDTH = N_HEADS * HEAD_DIM
KV_WIDTH = N_KV_HEADS * HEAD_DIM
ROT_DIM = HEAD_DIM // 4
ROPE_THETA = 500000.0
BLOCK = 128
PLE_DIM = 256

LANES = 128
SUBLANES = 8
BF16_ROWS = 16
VMEM_LIMIT = 56 * 1024 * 1024
LOG2E = math.log2(math.e)

HEAD_PERM = (0, 4, 1, 5, 2, 6, 3, 7, 8, 12, 9, 13, 10, 14, 11, 15)

Z0, Z1 = 0, SSD_WIDTH
X0, X1 = Z1, Z1 + XBC_WIDTH
Q0, Q1 = X1, X1 + ATTN_WIDTH
K0, K1 = Q1, Q1 + KV_WIDTH
V0, V1 = K1, K1 + KV_WIDTH
G0, G1 = V1, V1 + ATTN_WIDTH


def _dot(a, b):
    return jnp.dot(a, b, preferred_element_type=F32)


def _dot_nt(a, b):
    return lax.dot_general(a, b, (((1,), (1,)), ((), ())), preferred_element_type=F32)


def _dot_tn(a, b):
    return lax.dot_general(a, b, (((0,), (0,)), ((), ())), preferred_element_type=F32)


def _const_spec(shape, single_buffer=False):
    nd = len(shape)
    kw = {"pipeline_mode": pl.Buffered(1)} if single_buffer else {}
    return pl.BlockSpec(shape, lambda *_: (0,) * nd, **kw)


def _silu(x):
    return x / (1.0 + jnp.exp(-x))


def _split3(a):
    hi = a.astype(BF16).astype(F32)
    r1 = a - hi
    mid = r1.astype(BF16).astype(F32)
    lo = (r1 - mid).astype(BF16).astype(F32)
    return hi, mid, lo


def _in_proj_kernel(x_ref, xp_ref, xn_ref, nw_ref, w_ref, wdt_ref, cw_ref, cb_ref, cos_ref, sina_ref,
                    sinb_ref, z_ref, xc_ref, dtf_ref, dtb_ref, q_ref, k_ref, v_ref, g_ref, *, tiles_per_seq):
    i = pl.program_id(0)
    tm = x_ref.shape[0]
    nw = nw_ref[...]

    def norm(x):
        ms = jnp.mean(x * x, axis=-1, keepdims=True)
        return x * lax.rsqrt(ms + EPS) * nw

    hx = jnp.concatenate([norm(x_ref[...]), norm(xp_ref[...]), norm(xn_ref[...])], axis=0).astype(BF16)
    h = hx[:tm]

    z_ref[...] = _dot(h, w_ref[:, Z0:Z1]).astype(BF16)
    v_ref[...] = _dot(h, w_ref[:, V0:V1]).astype(BF16)
    g_ref[...] = _dot(h, w_ref[:, G0:G1]).astype(BF16)
    dt = _dot(h, wdt_ref[...])
    dtf_ref[...] = dt[:, :LANES]
    dtb_ref[...] = dt[:, LANES:]

    pre_all = _dot(hx, w_ref[:, X0:X1])
    pre = pre_all[:tm]
    first = (i % tiles_per_seq) == 0
    last = (i % tiles_per_seq) == tiles_per_seq - 1
    prow = jnp.where(first, 0.0, pre_all[tm + SUBLANES - 1:tm + SUBLANES, :])
    nrow = jnp.where(last, 0.0, pre_all[tm + SUBLANES:tm + SUBLANES + 1, :])
    rows = lax.broadcasted_iota(jnp.int32, (tm, 1), 0)
    up = jnp.where(rows == 0, prow, pltpu.roll(pre, 1, 0))
    dn = jnp.where(rows == tm - 1, nrow, pltpu.roll(pre, tm - 1, 0))
    cw = cw_ref[...]
    acc = up * cw[0:1, :] + pre * cw[1:2, :] + dn * cw[2:3, :] + cb_ref[...]
    xc_ref[...] = _silu(acc).astype(BF16)

    cosf = cos_ref[...]
    sina = sina_ref[...]
    sinb = sinb_ref[...]

    def rope(t):
        return (t * cosf + pltpu.roll(t, LANES - ROT_DIM // 2, 1) * sina
                + pltpu.roll(t, ROT_DIM // 2, 1) * sinb)

    q = _dot(h, w_ref[:, Q0:Q1]) * (HEAD_DIM ** -0.5)
    for j in range(ATTN_WIDTH // LANES):
        q_ref[:, j * LANES:(j + 1) * LANES] = rope(q[:, j * LANES:(j + 1) * LANES]).astype(BF16)
    k = _dot(h, w_ref[:, K0:K1])
    for j in range(KV_WIDTH // LANES):
        k_ref[:, j * LANES:(j + 1) * LANES] = rope(k[:, j * LANES:(j + 1) * LANES]).astype(BF16)


def _in_proj(x2, norm_w, w_main, w_dt, conv_w, conv_b, cosf, sina, sinb, seq, tm):
    t = x2.shape[0]
    tiles_per_seq = seq // tm
    rb = tm // SUBLANES
    nrb = t // SUBLANES
    row = lambda i: (i, 0)
    prev = lambda i: (jnp.maximum(i * rb - 1, 0), 0)
    nxt = lambda i: (jnp.minimum((i + 1) * rb, nrb - 1), 0)
    tab = lambda i: (i % tiles_per_seq, 0)
    out_shapes = (
        jax.ShapeDtypeStruct((t, SSD_WIDTH), BF16),
        jax.ShapeDtypeStruct((t, XBC_WIDTH), BF16),
        jax.ShapeDtypeStruct((t, LANES), F32),
        jax.ShapeDtypeStruct((t, LANES), F32),
        jax.ShapeDtypeStruct((t, ATTN_WIDTH), BF16),
        jax.ShapeDtypeStruct((t, KV_WIDTH), BF16),
        jax.ShapeDtypeStruct((t, KV_WIDTH), BF16),
        jax.ShapeDtypeStruct((t, ATTN_WIDTH), BF16),
    )
    return pl.pallas_call(
        functools.partial(_in_proj_kernel, tiles_per_seq=tiles_per_seq),
        grid=(t // tm,),
        in_specs=[
            pl.BlockSpec((tm, D_MODEL), row),
            pl.BlockSpec((SUBLANES, D_MODEL), prev),
            pl.BlockSpec((SUBLANES, D_MODEL), nxt),
            _const_spec((1, D_MODEL)),
            _const_spec(w_main.shape, single_buffer=True),
            _const_spec(w_dt.shape, single_buffer=True),
            _const_spec(conv_w.shape),
            _const_spec(conv_b.shape),
            pl.BlockSpec((tm, LANES), tab),
            pl.BlockSpec((tm, LANES), tab),
            pl.BlockSpec((tm, LANES), tab),
        ],
        out_specs=tuple(pl.BlockSpec((tm, s.shape[1]), row) for s in out_shapes),
        out_shape=out_shapes,
        compiler_params=pltpu.CompilerParams(
            dimension_semantics=("arbitrary",), vmem_limit_bytes=VMEM_LIMIT),
        name="in_proj",
    )(x2, x2, x2, norm_w, w_main, w_dt, conv_w, conv_b, cosf, sina, sinb)


def _ssd_prologue(xc_ref, dt_ref, bias, a_neg, e_bf, h_ref, reverse):
    xs_b = xc_ref[:, :SSD_WIDTH]
    bm_b = xc_ref[:, SSD_WIDTH:SSD_WIDTH + BC_WIDTH]
    cm_b = xc_ref[:, SSD_WIDTH + BC_WIDTH:]

    dtr = dt_ref[...] + bias
    dt = jnp.maximum(dtr, 0.0) + jnp.log1p(jnp.exp(-jnp.abs(dtr)))
    a = dt * a_neg
    ri = lax.broadcasted_iota(jnp.int32, (CHUNK, CHUNK), 0)
    ci = lax.broadcasted_iota(jnp.int32, (CHUNK, CHUNK), 1)
    tri = (ci >= ri) if reverse else (ci <= ri)
    tri_b = jnp.where(tri, 1.0, 0.0).astype(BF16)
    a_hi, a_mid, a_lo = _split3(a)
    acum = _dot(jnp.concatenate([tri_b, tri_b, tri_b], axis=1),
                jnp.concatenate([a_hi, a_mid, a_lo], axis=0).astype(BF16))
    acum2 = acum * LOG2E
    acum2_t = (acum2 - jnp.log2(dt)).T
    last = 0 if reverse else CHUNK - 1
    a2_last = acum2[last:last + 1, :]
    dte = jnp.exp2(a2_last - acum2)
    ecol = jnp.exp2(acum2)
    l_hi, l_mid, l_lo = _split3(jnp.broadcast_to(a2_last, (BF16_ROWS, LANES)))
    exp_in = jnp.concatenate([dte * dt, l_hi, l_mid, l_lo], axis=0).astype(BF16)
    exp_out = _dot(exp_in, e_bf)
    sc_exp = exp_out[:CHUNK]
    r0 = CHUNK
    a2_last_exp = (exp_out[r0:r0 + 1] + exp_out[r0 + BF16_ROWS:r0 + BF16_ROWS + 1]
                   + exp_out[r0 + 2 * BF16_ROWS:r0 + 2 * BF16_ROWS + 1])
    cd_exp = jnp.exp2(a2_last_exp)
    xsc = xs_b * sc_exp.astype(BF16)

    gmat = [_dot_nt(cm_b[:, g * D_STATE:(g + 1) * D_STATE], bm_b[:, g * D_STATE:(g + 1) * D_STATE])
            for g in range(SSD_GROUPS)]
    hprev = [h_ref[g] for g in range(SSD_GROUPS)]
    hprev_b = [h.astype(BF16) for h in hprev]
    return dict(xs_b=xs_b, bm_b=bm_b, cm_b=cm_b, tri=tri, acum2=acum2, acum2_t=acum2_t, ecol=ecol,
                cd_exp=cd_exp, xsc=xsc, gmat=gmat, hprev=hprev, hprev_b=hprev_b)


def _ssd_pair(pro, j):
    pairs_per_group = SSD_HEADS // SSD_GROUPS // 2
    g = j // pairs_per_group
    jj = j % pairs_per_group
    low = lax.broadcasted_iota(jnp.int32, (CHUNK, LANES), 1) < SSD_HEAD_DIM
    zero_b = jnp.zeros((CHUNK, LANES), BF16)
    xs_pair = pro["xs_b"][:, j * LANES:(j + 1) * LANES]
    h_pair = pro["hprev_b"][g][:, jj * LANES:(jj + 1) * LANES]
    cg = pro["cm_b"][:, g * D_STATE:(g + 1) * D_STATE].astype(F32)
    acc_y = None
    for e in range(2):
        hd = 2 * j + e
        seg = pro["acum2"][:, hd:hd + 1] - pro["acum2_t"][hd:hd + 1, :]
        dec = jnp.exp2(jnp.where(pro["tri"], seg, -jnp.inf))
        m = (pro["gmat"][g] * dec).astype(BF16)
        cs = (cg * pro["ecol"][:, hd:hd + 1]).astype(BF16)
        lhs = jnp.concatenate([m, cs], axis=1)
        if e == 0:
            rhs = jnp.concatenate([jnp.where(low, xs_pair, zero_b), jnp.where(low, h_pair, zero_b)], axis=0)
        else:
            rhs = jnp.concatenate([jnp.where(low, zero_b, xs_pair), jnp.where(low, zero_b, h_pair)], axis=0)
        r = _dot(lhs, rhs)
        acc_y = r if acc_y is None else acc_y + r
    return acc_y


def _ssd_state_update(pro, h_ref):
    gw = (SSD_HEADS // SSD_GROUPS) * SSD_HEAD_DIM
    for g in range(SSD_GROUPS):
        s_new = _dot_tn(pro["bm_b"][:, g * D_STATE:(g + 1) * D_STATE], pro["xsc"][:, g * gw:(g + 1) * gw])
        h_ref[g] = pro["hprev"][g] * pro["cd_exp"][:, g * gw:(g + 1) * gw] + s_new


def _ssd_kernel(xf_ref, xb_ref, dtf_ref, dtb_ref, bias_ref, alog_ref, dskip_ref, ebf_ref,
                yf_ref, yb_ref, hf_ref, hb_ref):
    c = pl.program_id(1)

    @pl.when(c == 0)
    def _():
        hf_ref[...] = jnp.zeros_like(hf_ref)
        hb_ref[...] = jnp.zeros_like(hb_ref)

    e_bf = ebf_ref[...]
    head_lane = lax.broadcasted_iota(jnp.int32, (1, LANES), 1) < SSD_HEADS
    a_neg = jnp.where(head_lane, -jnp.exp(alog_ref[...]), 0.0)
    bias = bias_ref[...]

    pf = _ssd_prologue(xf_ref, dtf_ref, bias[0:1], a_neg[0:1], e_bf, hf_ref, reverse=False)
    pb = _ssd_prologue(xb_ref, dtb_ref, bias[1:2], a_neg[1:2], e_bf, hb_ref, reverse=True)
    for j in range(SSD_HEADS // 2):
        sl = slice(j * LANES, (j + 1) * LANES)
        yf = _ssd_pair(pf, j)
        yf_ref[:, sl] = (yf + dskip_ref[:, sl] * pf["xs_b"][:, sl].astype(F32)).astype(BF16)
        yb_ref[:, sl] = _ssd_pair(pb, j).astype(BF16)
    _ssd_state_update(pf, hf_ref)
    _ssd_state_update(pb, hb_ref)


def _ssd(xc, dtf, dtb, bias2, alog2, dskip_exp, e_bf, batch, seq):
    t = xc.shape[0]
    nc = seq // CHUNK
    fw = lambda b, c: (b * nc + c, 0)
    bw = lambda b, c: (b * nc + nc - 1 - c, 0)
    gw = (SSD_HEADS // SSD_GROUPS) * SSD_HEAD_DIM
    return pl.pallas_call(
        _ssd_kernel,
        grid=(batch, nc),
        in_specs=[
            pl.BlockSpec((CHUNK, XBC_WIDTH), fw),
            pl.BlockSpec((CHUNK, XBC_WIDTH), bw),
            pl.BlockSpec((CHUNK, LANES), fw),
            pl.BlockSpec((CHUNK, LANES), bw),
            _const_spec(bias2.shape),
            _const_spec(alog2.shape),
            _const_spec(dskip_exp.shape),
            _const_spec(e_bf.shape),
        ],
        out_specs=(pl.BlockSpec((CHUNK, SSD_WIDTH), fw), pl.BlockSpec((CHUNK, SSD_WIDTH), bw)),
        out_shape=(jax.ShapeDtypeStruct((t, SSD_WIDTH), BF16),
                   jax.ShapeDtypeStruct((t, SSD_WIDTH), BF16)),
        scratch_shapes=[pltpu.VMEM((SSD_GROUPS, D_STATE, gw), F32),
                        pltpu.VMEM((SSD_GROUPS, D_STATE, gw), F32)],
        compiler_params=pltpu.CompilerParams(
            dimension_semantics=("arbitrary", "arbitrary"), vmem_limit_bytes=VMEM_LIMIT),
        name="ssd",
    )(xc, xc, dtf, dtb, bias2, alog2, dskip_exp, e_bf)


def _attn_kernel(sink_ref, q_ref, kp_ref, kc_ref, kn_ref, vp_ref, vc_ref, vn_ref, g_ref, nw_ref, o_ref):
    nb = pl.program_id(1)
    nblk = pl.num_programs(1)
    q = q_ref[...]
    kcat = jnp.concatenate([kp_ref[...], kc_ref[...], kn_ref[...]], axis=0)
    vcat = jnp.concatenate([vp_ref[...], vc_ref[...], vn_ref[...]], axis=0)

    r = lax.broadcasted_iota(jnp.int32, (BLOCK, 3 * BLOCK), 0)
    cc = lax.broadcasted_iota(jnp.int32, (BLOCK, 3 * BLOCK), 1)
    lo_lim = jnp.where(nb > 0, 0, BLOCK)
    hi_lim = jnp.where(nb < nblk - 1, 3 * BLOCK, 2 * BLOCK)
    valid = (cc >= r) & (cc <= r + 2 * BLOCK) & (cc >= lo_lim) & (cc < hi_lim)
    bias = jnp.where(valid, 0.0, -jnp.inf).astype(F32)

    low = lax.broadcasted_iota(jnp.int32, (BLOCK, LANES), 1) < HEAD_DIM
    low3 = lax.broadcasted_iota(jnp.int32, (3 * BLOCK, LANES), 1) < HEAD_DIM
    zq = jnp.zeros((BLOCK, LANES), BF16)
    tiles_per_pair = (N_HEADS // N_KV_HEADS)
    outs = []
    for j in range(N_KV_HEADS // 2):
        k_tile = kcat[:, j * LANES:(j + 1) * LANES]
        v_tile = vcat[:, j * LANES:(j + 1) * LANES]
        v_lo = jnp.where(low3, v_tile, jnp.zeros_like(v_tile))
        v_hi = jnp.where(low3, jnp.zeros_like(v_tile), v_tile)
        v2 = jnp.concatenate([v_lo, v_hi], axis=0)
        lhs_parts = []
        for tt in range(tiles_per_pair):
            qt = q[:, (j * tiles_per_pair + tt) * LANES:(j * tiles_per_pair + tt + 1) * LANES]
            lhs_parts.append(jnp.where(low, qt, zq))
            lhs_parts.append(jnp.where(low, zq, qt))
        s_all = _dot_nt(jnp.concatenate(lhs_parts, axis=0), k_tile)
        for tt in range(tiles_per_pair):
            ps = []
            invs = []
            for e in range(2):
                idx = 2 * tt + e
                head = HEAD_PERM[2 * (j * tiles_per_pair + tt) + e]
                sink = sink_ref[head]
                s = s_all[idx * BLOCK:(idx + 1) * BLOCK, :] + bias
                m = jnp.maximum(jnp.max(s, axis=-1, keepdims=True), sink)
                p = jnp.exp(s - m)
                den = jnp.sum(p, axis=-1, keepdims=True) + jnp.exp(sink - m)
                ps.append(p.astype(BF16))
                invs.append(1.0 / den)
            o_t = _dot(jnp.concatenate(ps, axis=1), v2)
            outs.append(o_t * jnp.where(low, invs[0], invs[1]))
    o = jnp.concatenate(outs, axis=1)
    y = o * _silu(g_ref[...].astype(F32))
    ms = jnp.mean(y * y, axis=-1, keepdims=True)
    o_ref[...] = (y * lax.rsqrt(ms + EPS) * nw_ref[...]).astype(BF16)


def _attn(sink, q, k, v, g, norm_w, batch, seq):
    t = q.shape[0]
    nblk = seq // BLOCK
    cur = lambda b, n: (b * nblk + n, 0)
    prev = lambda b, n: (b * nblk + jnp.maximum(n - 1, 0), 0)
    nxt = lambda b, n: (b * nblk + jnp.minimum(n + 1, nblk - 1), 0)
    return pl.pallas_call(
        _attn_kernel,
        grid=(batch, nblk),
        in_specs=[
            pl.BlockSpec(memory_space=pltpu.SMEM),
            pl.BlockSpec((BLOCK, ATTN_WIDTH), cur),
            pl.BlockSpec((BLOCK, KV_WIDTH), prev),
            pl.BlockSpec((BLOCK, KV_WIDTH), cur),
            pl.BlockSpec((BLOCK, KV_WIDTH), nxt),
            pl.BlockSpec((BLOCK, KV_WIDTH), prev),
            pl.BlockSpec((BLOCK, KV_WIDTH), cur),
            pl.BlockSpec((BLOCK, KV_WIDTH), nxt),
            pl.BlockSpec((BLOCK, ATTN_WIDTH), cur),
            _const_spec(norm_w.shape),
        ],
        out_specs=pl.BlockSpec((BLOCK, ATTN_WIDTH), cur),
        out_shape=jax.ShapeDtypeStruct((t, ATTN_WIDTH), BF16),
        compiler_params=pltpu.CompilerParams(
            dimension_semantics=("arbitrary", "arbitrary"), vmem_limit_bytes=VMEM_LIMIT),
        name="attn",
    )(sink, q, k, k, k, v, v, v, g, norm_w)


def _out_kernel(x_ref, yf_ref, yb_ref, z_ref, ya_ref, p_ref, snw_ref, wo_ref, wg_ref, bg_ref, wp_ref,
                fnw_ref, o_ref):
    y = yf_ref[...].astype(F32) + yb_ref[...].astype(F32)
    y = y * _silu(z_ref[...].astype(F32))
    ms = jnp.mean(y * y, axis=-1, keepdims=True)
    y_ssd = (y * lax.rsqrt(ms + EPS) * snw_ref[...]).astype(BF16)
    x1 = (x_ref[...] + _dot(y_ssd, wo_ref[:SSD_WIDTH, :]) + _dot(ya_ref[...], wo_ref[SSD_WIDTH:, :]))
    gate_lin = _dot(x1.astype(BF16), wg_ref[...]) + bg_ref[...]
    gate = 1.0 / (1.0 + jnp.exp(-gate_lin))
    x2 = x1 + gate * _dot(p_ref[...].astype(BF16), wp_ref[...])
    ms2 = jnp.mean(x2 * x2, axis=-1, keepdims=True)
    o_ref[...] = x2 * lax.rsqrt(ms2 + EPS) * fnw_ref[...]


def _out(x2, yf, yb, z, ya, p2, ssd_norm_w, w_out, w_gate, b_gate, w_ple, final_norm_w, tm):
    t = x2.shape[0]
    row = lambda i: (i, 0)
    return pl.pallas_call(
        _out_kernel,
        grid=(t // tm,),
        in_specs=[
            pl.BlockSpec((tm, D_MODEL), row),
            pl.BlockSpec((tm, SSD_WIDTH), row),
            pl.BlockSpec((tm, SSD_WIDTH), row),
            pl.BlockSpec((tm, SSD_WIDTH), row),
            pl.BlockSpec((tm, ATTN_WIDTH), row),
            pl.BlockSpec((tm, PLE_DIM), row),
            _const_spec(ssd_norm_w.shape),
            _const_spec(w_out.shape, single_buffer=True),
            _const_spec(w_gate.shape, single_buffer=True),
            _const_spec(b_gate.shape),
            _const_spec(w_ple.shape, single_buffer=True),
            _const_spec(final_norm_w.shape),
        ],
        out_specs=pl.BlockSpec((tm, D_MODEL), row),
        out_shape=jax.ShapeDtypeStruct((t, D_MODEL), F32),
        compiler_params=pltpu.CompilerParams(
            dimension_semantics=("arbitrary",), vmem_limit_bytes=VMEM_LIMIT),
        name="out_stage",
    )(x2, yf, yb, z, ya, p2, ssd_norm_w, w_out, w_gate, b_gate, w_ple, final_norm_w)


def _rope_tables(seq):
    inv_freq = ROPE_THETA ** (-jnp.arange(0, ROT_DIM, 2, dtype=F32) / ROT_DIM)
    ang = jnp.arange(seq, dtype=F32)[:, None] * inv_freq[None, :]
    cos, sin = jnp.cos(ang), jnp.sin(ang)
    half = ROT_DIM // 2
    lane = np.arange(LANES) % HEAD_DIM
    idx = np.where(lane < half, lane, np.where(lane < ROT_DIM, lane - half, 0))
    first = jnp.asarray(lane < half)
    second = jnp.asarray((lane >= half) & (lane < ROT_DIM))
    cos_l = cos[:, idx]
    sin_l = sin[:, idx]
    cosf = jnp.where(first | second, cos_l, 1.0)
    sina = jnp.where(first, -sin_l, 0.0)
    sinb = jnp.where(second, sin_l, 0.0)
    return cosf, sina, sinb


def _perm_heads(w, axis):
    parts = [lax.slice_in_dim(w, h * HEAD_DIM, (h + 1) * HEAD_DIM, axis=axis) for h in HEAD_PERM]
    return jnp.concatenate(parts, axis=axis)


def kernel(x, p, norm_w, w_in, conv_w, conv_b, dt_bias_f, dt_bias_b, a_log_f, a_log_b, d_skip, ssd_norm_w,
           attn_sink, attn_norm_w, w_out, ple_proj, ple_gate_w, ple_gate_b, final_norm_w):
    batch, seq, _ = x.shape
    assert p.shape[0] == 1, "single-layer problem (DEPTH == 1)"
    t = batch * seq
    tm_in = min(512, seq)
    tm_out = min(256, seq)

    w = w_in[0]
    split = np.cumsum((SSD_WIDTH, XBC_WIDTH, 2 * SSD_HEADS, ATTN_WIDTH, KV_WIDTH, KV_WIDTH, ATTN_WIDTH))
    w_main = jnp.concatenate([
        w[:, :split[1]],
        _perm_heads(w[:, split[2]:split[3]], 1),
        w[:, split[3]:split[5]],
        _perm_heads(w[:, split[5]:split[6]], 1)], axis=1).astype(BF16)
    w_dt = w[:, split[1]:split[2]]
    pad = jnp.zeros((D_MODEL, LANES - SSD_HEADS), w.dtype)
    w_dt2 = jnp.concatenate([w_dt[:, :SSD_HEADS], pad, w_dt[:, SSD_HEADS:], pad], axis=1).astype(BF16)

    x2 = x.reshape(t, D_MODEL)
    z, xc, dtf, dtb, q, k, v, g = _in_proj(
        x2, norm_w[0][None, :], w_main, w_dt2, conv_w[0], conv_b[0][None, :], *_rope_tables(seq), seq, tm_in)

    lane_pad = jnp.zeros((LANES - SSD_HEADS,), F32)
    bias2 = jnp.stack([jnp.concatenate([dt_bias_f[0], lane_pad]), jnp.concatenate([dt_bias_b[0], lane_pad])])
    alog2 = jnp.stack([jnp.concatenate([a_log_f[0], lane_pad]), jnp.concatenate([a_log_b[0], lane_pad])])
    dskip_exp = jnp.repeat(d_skip[0], SSD_HEAD_DIM)[None, :]
    head_of_lane = np.arange(SSD_WIDTH) // SSD_HEAD_DIM
    e_bf = jnp.asarray((np.arange(LANES)[:, None] == head_of_lane[None, :]).astype(np.float32), BF16)
    yf, yb = _ssd(xc, dtf, dtb, bias2, alog2, dskip_exp, e_bf, batch, seq)

    ya = _attn(attn_sink[0], q, k, v, g, _perm_heads(attn_norm_w[0], 0)[None, :], batch, seq)

    w_out_b = jnp.concatenate([w_out[0][:SSD_WIDTH], _perm_heads(w_out[0][SSD_WIDTH:], 0)], axis=0).astype(BF16)
    out = _out(x2, yf, yb, z, ya, p[0].reshape(t, PLE_DIM), ssd_norm_w[0][None, :], w_out_b,
               ple_gate_w[0].astype(BF16), ple_gate_b[0][None, :], ple_proj[0].astype(BF16),
               final_norm_w[None, :], tm_out)
    return out.reshape(batch, seq, D_MODEL)
```

```python
import functools
import math

import numpy as np
import jax
import jax.numpy as jnp
from jax import lax
from jax.experimental import pallas as pl
from jax.experimental.pallas import tpu as pltpu

F32 = jnp.float32
BF16 = jnp.bfloat16

EPS = 1e-6
D_MODEL = 1024
SSD_HEADS = 16
SSD_HEAD_DIM = 64
SSD_WIDTH = SSD_HEADS * SSD_HEAD_DIM
SSD_GROUPS = 2
D_STATE = 128
BC_WIDTH = SSD_GROUPS * D_STATE
XBC_WIDTH = SSD_WIDTH + 2 * BC_WIDTH
D_CONV = 3
CHUNK = 128
HEAD_DIM = 64
N_HEADS = 16
N_KV_HEADS = 4
ATTN_WIDTH = N_HEADS * HEAD_DIM
KV_WIDTH = N_KV_HEADS * HEAD_DIM
ROT_DIM = HEAD_DIM // 4
ROPE_THETA = 500000.0
BLOCK = 128
PLE_DIM = 256

LANES = 128
SUBLANES = 8
BF16_ROWS = 16
VMEM_LIMIT = 56 * 1024 * 1024
LOG2E = math.log2(math.e)

HEAD_PERM = (0, 4, 1, 5, 2, 6, 3, 7, 8, 12, 9, 13, 10, 14, 11, 15)

Z0, Z1 = 0, SSD_WIDTH
X0, X1 = Z1, Z1 + XBC_WIDTH
Q0, Q1 = X1, X1 + ATTN_WIDTH
K0, K1 = Q1, Q1 + KV_WIDTH
V0, V1 = K1, K1 + KV_WIDTH
G0, G1 = V1, V1 + ATTN_WIDTH


def _dot(a, b):
    return jnp.dot(a, b, preferred_element_type=F32)


def _dot_nt(a, b):
    return lax.dot_general(a, b, (((1,), (1,)), ((), ())), preferred_element_type=F32)


def _dot_tn(a, b):
    return lax.dot_general(a, b, (((0,), (0,)), ((), ())), preferred_element_type=F32)


def _const_spec(shape, single_buffer=False):
    nd = len(shape)
    kw = {"pipeline_mode": pl.Buffered(1)} if single_buffer else {}
    return pl.BlockSpec(shape, lambda *_: (0,) * nd, **kw)


def _silu(x):
    return x / (1.0 + jnp.exp(-x))


def _split3(a):
    hi = a.astype(BF16).astype(F32)
    r1 = a - hi
    mid = r1.astype(BF16).astype(F32)
    lo = (r1 - mid).astype(BF16).astype(F32)
    return hi, mid, lo


def _in_proj_kernel(x_ref, xp_ref, xn_ref, nw_ref, w_ref, wdt_ref, cw_ref, cb_ref, cos_ref, sina_ref,
                    sinb_ref, dtb_ref, alog_ref, tri_ref, z_ref, xc_ref, a2_ref, d2_ref, d2t_ref, q_ref, k_ref,
                    v_ref, g_ref, *, tiles_per_seq):
    i = pl.program_id(0)
    tm = x_ref.shape[0]
    nw = nw_ref[...]

    def norm(x):
        ms = jnp.mean(x * x, axis=-1, keepdims=True)
        return x * lax.rsqrt(ms + EPS) * nw

    hx = jnp.concatenate([norm(x_ref[...]), norm(xp_ref[...]), norm(xn_ref[...])], axis=0).astype(BF16)
    h = hx[:tm]

    z_ref[...] = _dot(h, w_ref[:, Z0:Z1]).astype(BF16)
    v_ref[...] = _dot(h, w_ref[:, V0:V1]).astype(BF16)
    g_ref[...] = _dot(h, w_ref[:, G0:G1]).astype(BF16)
    dtr = _dot(h, wdt_ref[...]) + dtb_ref[...]
    dt = jnp.maximum(dtr, 0.0) + jnp.log1p(jnp.exp(-jnp.abs(dtr)))
    lane = lax.broadcasted_iota(jnp.int32, (1, LANES), 1)
    a = dt * jnp.where(lane < 2 * SSD_HEADS, -jnp.exp(alog_ref[...]), 0.0)
    ldt = jnp.log2(dt)
    fwd_lane = lane < SSD_HEADS
    tri2 = tri_ref[...]
    for ci in range(tm // CHUNK):
        rs = slice(ci * CHUNK, (ci + 1) * CHUNK)
        a_hi, a_mid, a_lo = _split3(a[rs])
        both = _dot(tri2, jnp.concatenate([a_hi, a_mid, a_lo], axis=0).astype(BF16))
        acum = jnp.where(fwd_lane, both[:CHUNK], both[CHUNK:])
        acum2 = acum * LOG2E
        d2 = acum2 - ldt[rs]
        a2_ref[rs, :] = acum2
        d2_ref[rs, :] = d2
        d2t_ref[ci] = d2.T[:2 * SSD_HEADS, :]

    pre_all = _dot(hx, w_ref[:, X0:X1])
    pre = pre_all[:tm]
    first = (i % tiles_per_seq) == 0
    last = (i % tiles_per_seq) == tiles_per_seq - 1
    prow = jnp.where(first, 0.0, pre_all[tm + SUBLANES - 1:tm + SUBLANES, :])
    nrow = jnp.where(last, 0.0, pre_all[tm + SUBLANES:tm + SUBLANES + 1, :])
    rows = lax.broadcasted_iota(jnp.int32, (tm, 1), 0)
    up = jnp.where(rows == 0, prow, pltpu.roll(pre, 1, 0))
    dn = jnp.where(rows == tm - 1, nrow, pltpu.roll(pre, tm - 1, 0))
    cw = cw_ref[...]
    acc = up * cw[0:1, :] + pre * cw[1:2, :] + dn * cw[2:3, :] + cb_ref[...]
    xc_ref[...] = _silu(acc).astype(BF16)

    cosf = cos_ref[...]
    sina = sina_ref[...]
    sinb = sinb_ref[...]

    def rope(t):
        return (t * cosf + pltpu.roll(t, LANES - ROT_DIM // 2, 1) * sina
                + pltpu.roll(t, ROT_DIM // 2, 1) * sinb)

    q = _dot(h, w_ref[:, Q0:Q1]) * (HEAD_DIM ** -0.5 * LOG2E)
    for j in range(ATTN_WIDTH // LANES):
        q_ref[:, j * LANES:(j + 1) * LANES] = rope(q[:, j * LANES:(j + 1) * LANES]).astype(BF16)
    k = _dot(h, w_ref[:, K0:K1])
    for j in range(KV_WIDTH // LANES):
        k_ref[:, j * LANES:(j + 1) * LANES] = rope(k[:, j * LANES:(j + 1) * LANES]).astype(BF16)


def _in_proj(x2, norm_w, w_main, w_dt, conv_w, conv_b, cosf, sina, sinb, dt_bias, a_log, seq, tm):
    t = x2.shape[0]
    tiles_per_seq = seq // tm
    cpt = tm // CHUNK
    rb = tm // SUBLANES
    nrb = t // SUBLANES
    row = lambda i: (i, 0)
    prev = lambda i: (jnp.maximum(i * rb - 1, 0), 0)
    nxt = lambda i: (jnp.minimum((i + 1) * rb, nrb - 1), 0)
    tab = lambda i: (i % tiles_per_seq, 0)
    ri = np.arange(CHUNK)[:, None]
    ci = np.arange(CHUNK)[None, :]
    tri2 = jnp.asarray(np.concatenate([np.tile(ci <= ri, (1, 3)), np.tile(ci >= ri, (1, 3))], axis=0), BF16)
    out_shapes = (
        jax.ShapeDtypeStruct((t, SSD_WIDTH), BF16),
        jax.ShapeDtypeStruct((t, XBC_WIDTH), BF16),
        jax.ShapeDtypeStruct((t, LANES), F32),
        jax.ShapeDtypeStruct((t, LANES), F32),
        jax.ShapeDtypeStruct((t // CHUNK, 2 * SSD_HEADS, CHUNK), F32),
        jax.ShapeDtypeStruct((t, ATTN_WIDTH), BF16),
        jax.ShapeDtypeStruct((t, KV_WIDTH), BF16),
        jax.ShapeDtypeStruct((t, KV_WIDTH), BF16),
        jax.ShapeDtypeStruct((t, ATTN_WIDTH), BF16),
    )
    out_specs = tuple(
        pl.BlockSpec((cpt,) + s.shape[1:], lambda i: (i, 0, 0)) if len(s.shape) == 3
        else pl.BlockSpec((tm, s.shape[1]), row) for s in out_shapes)
    return pl.pallas_call(
        functools.partial(_in_proj_kernel, tiles_per_seq=tiles_per_seq),
        grid=(t // tm,),
        in_specs=[
            pl.BlockSpec((tm, D_MODEL), row),
            pl.BlockSpec((SUBLANES, D_MODEL), prev),
            pl.BlockSpec((SUBLANES, D_MODEL), nxt),
            _const_spec((1, D_MODEL)),
            _const_spec(w_main.shape, single_buffer=True),
            _const_spec(w_dt.shape, single_buffer=True),
            _const_spec(conv_w.shape),
            _const_spec(conv_b.shape),
            pl.BlockSpec((tm, LANES), tab),
            pl.BlockSpec((tm, LANES), tab),
            pl.BlockSpec((tm, LANES), tab),
            _const_spec(dt_bias.shape),
            _const_spec(a_log.shape),
            _const_spec(tri2.shape),
        ],
        out_specs=out_specs,
        out_shape=out_shapes,
        compiler_params=pltpu.CompilerParams(
            dimension_semantics=("arbitrary",), vmem_limit_bytes=VMEM_LIMIT),
        name="in_proj",
    )(x2, x2, x2, norm_w, w_main, w_dt, conv_w, conv_b, cosf, sina, sinb, dt_bias, a_log, tri2)


def _ssd_prologue(xc_ref, a2_ref, d2_ref, d2t_ref, e_bf, h_ref, reverse):
    xs_b = xc_ref[:, :SSD_WIDTH]
    bm_b = xc_ref[:, SSD_WIDTH:SSD_WIDTH + BC_WIDTH]
    cm_b = xc_ref[:, SSD_WIDTH + BC_WIDTH:]

    ri = lax.broadcasted_iota(jnp.int32, (CHUNK, CHUNK), 0)
    ci = lax.broadcasted_iota(jnp.int32, (CHUNK, CHUNK), 1)
    tri = (ci >= ri) if reverse else (ci <= ri)
    acum2 = a2_ref[...]
    last = 0 if reverse else CHUNK - 1
    a2_last = acum2[last:last + 1, :]
    lane0 = SSD_HEADS if reverse else 0
    lane = lax.broadcasted_iota(jnp.int32, (1, LANES), 1)
    mine = (lane >= lane0) & (lane < lane0 + SSD_HEADS)
    sc = jnp.where(mine, jnp.exp2(a2_last - d2_ref[...]), 0.0)
    l_hi, l_mid, l_lo = _split3(jnp.broadcast_to(a2_last, (BF16_ROWS, LANES)))
    exp_in = jnp.concatenate([sc, l_hi, l_mid, l_lo], axis=0).astype(BF16)
    exp_out = _dot(exp_in, e_bf)
    sc_exp = exp_out[:CHUNK]
    r0 = CHUNK
    a2_last_exp = (exp_out[r0:r0 + 1] + exp_out[r0 + BF16_ROWS:r0 + BF16_ROWS + 1]
                   + exp_out[r0 + 2 * BF16_ROWS:r0 + 2 * BF16_ROWS + 1])
    cd_exp = jnp.exp2(a2_last_exp)
    xsc = xs_b * sc_exp.astype(BF16)

    gmat = [_dot_nt(cm_b[:, g * D_STATE:(g + 1) * D_STATE], bm_b[:, g * D_STATE:(g + 1) * D_STATE])
            for g in range(SSD_GROUPS)]
    hprev = [h_ref[g] for g in range(SSD_GROUPS)]
    hprev_b = [h.astype(BF16) for h in hprev]
    return dict(xs_b=xs_b, bm_b=bm_b, cm_b=cm_b, tri=tri, acum2=acum2, d2t=d2t_ref[0],
                lane0=lane0, cd_exp=cd_exp, xsc=xsc, gmat=gmat, hprev=hprev,
                hprev_b=hprev_b)


def _ssd_pair(pro, j):
    pairs_per_group = SSD_HEADS // SSD_GROUPS // 2
    g = j // pairs_per_group
    jj = j % pairs_per_group
    low = lax.broadcasted_iota(jnp.int32, (CHUNK, LANES), 1) < SSD_HEAD_DIM
    zero_b = jnp.zeros((CHUNK, LANES), BF16)
    xs_pair = pro["xs_b"][:, j * LANES:(j + 1) * LANES]
    h_pair = pro["hprev_b"][g][:, jj * LANES:(jj + 1) * LANES]
    cg = pro["cm_b"][:, g * D_STATE:(g + 1) * D_STATE].astype(F32)
    acc_y = None
    for e in range(2):
        hd = pro["lane0"] + 2 * j + e
        col = jnp.broadcast_to(pro["acum2"][:, hd:hd + 1], (CHUNK, CHUNK))
        seg = col - pro["d2t"][hd:hd + 1, :]
        dec = jnp.exp2(jnp.where(pro["tri"], seg, -jnp.inf))
        m = (pro["gmat"][g] * dec).astype(BF16)
        cs = (cg * jnp.exp2(col)).astype(BF16)
        lhs = jnp.concatenate([m, cs], axis=1)
        if e == 0:
            rhs = jnp.concatenate([jnp.where(low, xs_pair, zero_b), jnp.where(low, h_pair, zero_b)], axis=0)
        else:
            rhs = jnp.concatenate([jnp.where(low, zero_b, xs_pair), jnp.where(low, zero_b, h_pair)], axis=0)
        r = _dot(lhs, rhs)
        acc_y = r if acc_y is None else acc_y + r
    return acc_y


def _ssd_state_update(pro, h_ref):
    gw = (SSD_HEADS // SSD_GROUPS) * SSD_HEAD_DIM
    for g in range(SSD_GROUPS):
        s_new = _dot_tn(pro["bm_b"][:, g * D_STATE:(g + 1) * D_STATE], pro["xsc"][:, g * gw:(g + 1) * gw])
        h_ref[g] = pro["hprev"][g] * pro["cd_exp"][:, g * gw:(g + 1) * gw] + s_new


def _ssd_kernel(xf_ref, xb_ref, a2f_ref, a2b_ref, d2f_ref, d2b_ref, d2tf_ref, d2tb_ref, dskip_ref,
                ef_ref, eb_ref, yf_ref, yb_ref, hf_ref, hb_ref):
    c = pl.program_id(1)

    @pl.when(c == 0)
    def _():
        hf_ref[...] = jnp.zeros_like(hf_ref)
        hb_ref[...] = jnp.zeros_like(hb_ref)

    pf = _ssd_prologue(xf_ref, a2f_ref, d2f_ref, d2tf_ref, ef_ref[...], hf_ref, reverse=False)
    pb = _ssd_prologue(xb_ref, a2b_ref, d2b_ref, d2tb_ref, eb_ref[...], hb_ref, reverse=True)
    for j in range(SSD_HEADS // 2):
        sl = slice(j * LANES, (j + 1) * LANES)
        yf = _ssd_pair(pf, j)
        yf_ref[:, sl] = (yf + dskip_ref[:, sl] * pf["xs_b"][:, sl].astype(F32)).astype(BF16)
        yb_ref[:, sl] = _ssd_pair(pb, j).astype(BF16)
    _ssd_state_update(pf, hf_ref)
    _ssd_state_update(pb, hb_ref)


def _ssd(xc, a2, d2, d2t, dskip_exp, e_f, e_b, batch, seq):
    t = xc.shape[0]
    nc = seq // CHUNK
    fw = lambda b, c: (b * nc + c, 0)
    bw = lambda b, c: (b * nc + nc - 1 - c, 0)
    fw3 = lambda b, c: (b * nc + c, 0, 0)
    bw3 = lambda b, c: (b * nc + nc - 1 - c, 0, 0)
    gw = (SSD_HEADS // SSD_GROUPS) * SSD_HEAD_DIM
    return pl.pallas_call(
        _ssd_kernel,
        grid=(batch, nc),
        in_specs=[
            pl.BlockSpec((CHUNK, XBC_WIDTH), fw),
            pl.BlockSpec((CHUNK, XBC_WIDTH), bw),
            pl.BlockSpec((CHUNK, LANES), fw),
            pl.BlockSpec((CHUNK, LANES), bw),
            pl.BlockSpec((CHUNK, LANES), fw),
            pl.BlockSpec((CHUNK, LANES), bw),
            pl.BlockSpec((1, 2 * SSD_HEADS, CHUNK), fw3),
            pl.BlockSpec((1, 2 * SSD_HEADS, CHUNK), bw3),
            _const_spec(dskip_exp.shape),
            _const_spec(e_f.shape),
            _const_spec(e_b.shape),
        ],
        out_specs=(pl.BlockSpec((CHUNK, SSD_WIDTH), fw), pl.BlockSpec((CHUNK, SSD_WIDTH), bw)),
        out_shape=(jax.ShapeDtypeStruct((t, SSD_WIDTH), BF16),
                   jax.ShapeDtypeStruct((t, SSD_WIDTH), BF16)),
        scratch_shapes=[pltpu.VMEM((SSD_GROUPS, D_STATE, gw), F32),
                        pltpu.VMEM((SSD_GROUPS, D_STATE, gw), F32)],
        compiler_params=pltpu.CompilerParams(
            dimension_semantics=("arbitrary", "arbitrary"), vmem_limit_bytes=VMEM_LIMIT),
        name="ssd",
    )(xc, xc, a2, a2, d2, d2, d2t, d2t, dskip_exp, e_f, e_b)


def _attn_kernel(sink_ref, q_ref, kp_ref, kc_ref, kn_ref, vp_ref, vc_ref, vn_ref, g_ref, nw_ref, o_ref):
    nb = pl.program_id(1)
    nblk = pl.num_programs(1)
    q = q_ref[...]
    kcat = jnp.concatenate([kp_ref[...], kc_ref[...], kn_ref[...]], axis=0)
    vcat = jnp.concatenate([vp_ref[...], vc_ref[...], vn_ref[...]], axis=0)

    r = lax.broadcasted_iota(jnp.int32, (BLOCK, BLOCK), 0)
    cc = lax.broadcasted_iota(jnp.int32, (BLOCK, BLOCK), 1)
    left_from = jnp.where(nb > 0, 0, BLOCK)
    right_to = jnp.where(nb < nblk - 1, 0, -BLOCK)
    bias_l = jnp.where(cc >= r + left_from, 0.0, -jnp.inf).astype(F32)
    bias_r = jnp.where(cc <= r + right_to, 0.0, -jnp.inf).astype(F32)

    low = lax.broadcasted_iota(jnp.int32, (BLOCK, LANES), 1) < HEAD_DIM
    low3 = lax.broadcasted_iota(jnp.int32, (3 * BLOCK, LANES), 1) < HEAD_DIM
    zq = jnp.zeros((BLOCK, LANES), BF16)
    tiles_per_pair = (N_HEADS // N_KV_HEADS)
    outs = []
    for j in range(N_KV_HEADS // 2):
        k_tile = kcat[:, j * LANES:(j + 1) * LANES]
        v_tile = vcat[:, j * LANES:(j + 1) * LANES]
        v_lo = jnp.where(low3, v_tile, jnp.zeros_like(v_tile))
        v_hi = jnp.where(low3, jnp.zeros_like(v_tile), v_tile)
        v2 = jnp.concatenate([v_lo, v_hi], axis=0)
        lhs_parts = []
        for tt in range(tiles_per_pair):
            qt = q[:, (j * tiles_per_pair + tt) * LANES:(j * tiles_per_pair + tt + 1) * LANES]
            lhs_parts.append(jnp.where(low, qt, zq))
            lhs_parts.append(jnp.where(low, zq, qt))
        s_all = _dot_nt(jnp.concatenate(lhs_parts, axis=0), k_tile)
        for tt in range(tiles_per_pair):
            ps = []
            invs = []
            for e in range(2):
                idx = 2 * tt + e
                head = HEAD_PERM[2 * (j * tiles_per_pair + tt) + e]
                sink = sink_ref[head] * LOG2E
                rows = slice(idx * BLOCK, (idx + 1) * BLOCK)
                s_l = s_all[rows, :BLOCK] + bias_l
                s_c = s_all[rows, BLOCK:2 * BLOCK]
                s_r = s_all[rows, 2 * BLOCK:] + bias_r
                m = jnp.max(jnp.maximum(jnp.maximum(s_l, s_c), s_r), axis=-1, keepdims=True)
                m = jnp.maximum(m, sink)
                p_l = jnp.exp2(s_l - m)
                p_c = jnp.exp2(s_c - m)
                p_r = jnp.exp2(s_r - m)
                den = jnp.sum(p_l + p_c + p_r, axis=-1, keepdims=True) + jnp.exp2(sink - m)
                ps += [p_l.astype(BF16), p_c.astype(BF16), p_r.astype(BF16)]
                invs.append(1.0 / den)
            o_t = _dot(jnp.concatenate(ps, axis=1), v2)
            outs.append(o_t * jnp.where(low, invs[0], invs[1]))
    o = jnp.concatenate(outs, axis=1)
    y = o * _silu(g_ref[...].astype(F32))
    ms = jnp.mean(y * y, axis=-1, keepdims=True)
    o_ref[...] = (y * lax.rsqrt(ms + EPS) * nw_ref[...]).astype(BF16)


def _attn(sink, q, k, v, g, norm_w, batch, seq):
    t = q.shape[0]
    nblk = seq // BLOCK
    cur = lambda b, n: (b * nblk + n, 0)
    prev = lambda b, n: (b * nblk + jnp.maximum(n - 1, 0), 0)
    nxt = lambda b, n: (b * nblk + jnp.minimum(n + 1, nblk - 1), 0)
    return pl.pallas_call(
        _attn_kernel,
        grid=(batch, nblk),
        in_specs=[
            pl.BlockSpec(memory_space=pltpu.SMEM),
            pl.BlockSpec((BLOCK, ATTN_WIDTH), cur),
            pl.BlockSpec((BLOCK, KV_WIDTH), prev),
            pl.BlockSpec((BLOCK, KV_WIDTH), cur),
            pl.BlockSpec((BLOCK, KV_WIDTH), nxt),
            pl.BlockSpec((BLOCK, KV_WIDTH), prev),
            pl.BlockSpec((BLOCK, KV_WIDTH), cur),
            pl.BlockSpec((BLOCK, KV_WIDTH), nxt),
            pl.BlockSpec((BLOCK, ATTN_WIDTH), cur),
            _const_spec(norm_w.shape),
        ],
        out_specs=pl.BlockSpec((BLOCK, ATTN_WIDTH), cur),
        out_shape=jax.ShapeDtypeStruct((t, ATTN_WIDTH), BF16),
        compiler_params=pltpu.CompilerParams(
            dimension_semantics=("arbitrary", "arbitrary"), vmem_limit_bytes=VMEM_LIMIT),
        name="attn",
    )(sink, q, k, k, k, v, v, v, g, norm_w)


def _out_kernel(x_ref, yf_ref, yb_ref, z_ref, ya_ref, p_ref, snw_ref, wo_ref, wg_ref, bg_ref, wp_ref,
                fnw_ref, o_ref):
    y = yf_ref[...].astype(F32) + yb_ref[...].astype(F32)
    y = y * _silu(z_ref[...].astype(F32))
    ms = jnp.mean(y * y, axis=-1, keepdims=True)
    y_ssd = (y * lax.rsqrt(ms + EPS) * snw_ref[...]).astype(BF16)
    x1 = (x_ref[...] + _dot(y_ssd, wo_ref[:SSD_WIDTH, :]) + _dot(ya_ref[...], wo_ref[SSD_WIDTH:, :]))
    gate_lin = _dot(x1.astype(BF16), wg_ref[...]) + bg_ref[...]
    gate = 1.0 / (1.0 + jnp.exp(-gate_lin))
    x2 = x1 + gate * _dot(p_ref[...].astype(BF16), wp_ref[...])
    ms2 = jnp.mean(x2 * x2, axis=-1, keepdims=True)
    o_ref[...] = x2 * lax.rsqrt(ms2 + EPS) * fnw_ref[...]


def _out(x2, yf, yb, z, ya, p2, ssd_norm_w, w_out, w_gate, b_gate, w_ple, final_norm_w, tm):
    t = x2.shape[0]
    row = lambda i: (i, 0)
    return pl.pallas_call(
        _out_kernel,
        grid=(t // tm,),
        in_specs=[
            pl.BlockSpec((tm, D_MODEL), row),
            pl.BlockSpec((tm, SSD_WIDTH), row),
            pl.BlockSpec((tm, SSD_WIDTH), row),
            pl.BlockSpec((tm, SSD_WIDTH), row),
            pl.BlockSpec((tm, ATTN_WIDTH), row),
            pl.BlockSpec((tm, PLE_DIM), row),
            _const_spec(ssd_norm_w.shape),
            _const_spec(w_out.shape, single_buffer=True),
            _const_spec(w_gate.shape, single_buffer=True),
            _const_spec(b_gate.shape),
            _const_spec(w_ple.shape, single_buffer=True),
            _const_spec(final_norm_w.shape),
        ],
        out_specs=pl.BlockSpec((tm, D_MODEL), row),
        out_shape=jax.ShapeDtypeStruct((t, D_MODEL), F32),
        compiler_params=pltpu.CompilerParams(
            dimension_semantics=("arbitrary",), vmem_limit_bytes=VMEM_LIMIT),
        name="out_stage",
    )(x2, yf, yb, z, ya, p2, ssd_norm_w, w_out, w_gate, b_gate, w_ple, final_norm_w)


def _rope_tables(seq):
    inv_freq = ROPE_THETA ** (-jnp.arange(0, ROT_DIM, 2, dtype=F32) / ROT_DIM)
    ang = jnp.arange(seq, dtype=F32)[:, None] * inv_freq[None, :]
    cos, sin = jnp.cos(ang), jnp.sin(ang)
    half = ROT_DIM // 2
    lane = np.arange(LANES) % HEAD_DIM
    idx = np.where(lane < half, lane, np.where(lane < ROT_DIM, lane - half, 0))
    first = jnp.asarray(lane < half)
    second = jnp.asarray((lane >= half) & (lane < ROT_DIM))
    cos_l = cos[:, idx]
    sin_l = sin[:, idx]
    cosf = jnp.where(first | second, cos_l, 1.0)
    sina = jnp.where(first, -sin_l, 0.0)
    sinb = jnp.where(second, sin_l, 0.0)
    return cosf, sina, sinb


def _perm_heads(w, axis):
    parts = [lax.slice_in_dim(w, h * HEAD_DIM, (h + 1) * HEAD_DIM, axis=axis) for h in HEAD_PERM]
    return jnp.concatenate(parts, axis=axis)


def kernel(x, p, norm_w, w_in, conv_w, conv_b, dt_bias_f, dt_bias_b, a_log_f, a_log_b, d_skip, ssd_norm_w,
           attn_sink, attn_norm_w, w_out, ple_proj, ple_gate_w, ple_gate_b, final_norm_w):
    batch, seq, _ = x.shape
    assert p.shape[0] == 1, "single-layer problem (DEPTH == 1)"
    t = batch * seq
    tm_in = min(512, seq)
    tm_out = min(256, seq)

    w = w_in[0]
    split = np.cumsum((SSD_WIDTH, XBC_WIDTH, 2 * SSD_HEADS, ATTN_WIDTH, KV_WIDTH, KV_WIDTH, ATTN_WIDTH))
    w_main = jnp.concatenate([
        w[:, :split[1]],
        _perm_heads(w[:, split[2]:split[3]], 1),
        w[:, split[3]:split[5]],
        _perm_heads(w[:, split[5]:split[6]], 1)], axis=1).astype(BF16)
    lane_pad = LANES - 2 * SSD_HEADS
    w_dt = jnp.pad(w[:, split[1]:split[2]], ((0, 0), (0, lane_pad))).astype(BF16)
    dt_bias = jnp.pad(jnp.concatenate([dt_bias_f[0], dt_bias_b[0]]), (0, lane_pad))[None, :]
    a_log = jnp.pad(jnp.concatenate([a_log_f[0], a_log_b[0]]), (0, lane_pad))[None, :]

    x2 = x.reshape(t, D_MODEL)
    z, xc, a2, d2, d2t, q, k, v, g = _in_proj(
        x2, norm_w[0][None, :], w_main, w_dt, conv_w[0], conv_b[0][None, :], *_rope_tables(seq),
        dt_bias, a_log, seq, tm_in)

    dskip_exp = jnp.repeat(d_skip[0], SSD_HEAD_DIM)[None, :]
    head_of_lane = np.arange(SSD_WIDTH) // SSD_HEAD_DIM
    e_np = (np.arange(LANES)[:, None] == head_of_lane[None, :]).astype(np.float32)
    e_f = jnp.asarray(e_np, BF16)
    e_b = jnp.asarray(np.roll(e_np, SSD_HEADS, axis=0), BF16)
    yf, yb = _ssd(xc, a2, d2, d2t, dskip_exp, e_f, e_b, batch, seq)

    ya = _attn(attn_sink[0], q, k, v, g, _perm_heads(attn_norm_w[0], 0)[None, :], batch, seq)

    w_out_b = jnp.concatenate([w_out[0][:SSD_WIDTH], _perm_heads(w_out[0][SSD_WIDTH:], 0)], axis=0).astype(BF16)
    out = _out(x2, yf, yb, z, ya, p[0].reshape(t, PLE_DIM), ssd_norm_w[0][None, :], w_out_b,
               ple_gate_w[0].astype(BF16), ple_gate_b[0][None, :], ple_proj[0].astype(BF16),
               final_norm_w[None, :], tm_out)
    return out.reshape(batch, seq, D_MODEL)
```

```python
import functools
import math

import numpy as np
import jax
import jax.numpy as jnp
from jax import lax
from jax.experimental import pallas as pl
from jax.experimental.pallas import tpu as pltpu

F32 = jnp.float32
BF16 = jnp.bfloat16

EPS = 1e-6
D_MODEL = 1024
SSD_HEADS = 16
SSD_HEAD_DIM = 64
SSD_WIDTH = SSD_HEADS * SSD_HEAD_DIM
SSD_GROUPS = 2
D_STATE = 128
BC_WIDTH = SSD_GROUPS * D_STATE
XBC_WIDTH = SSD_WIDTH + 2 * BC_WIDTH
D_CONV = 3
CHUNK = 128
HEAD_DIM = 64
N_HEADS = 16
N_KV_HEADS = 4
ATTN_WIDTH = N_HEADS * HEAD_DIM
KV_WIDTH = N_KV_HEADS * HEAD_DIM
ROT_DIM = HEAD_DIM // 4
ROPE_THETA = 500000.0
BLOCK = 128
PLE_DIM = 256

LANES = 128
SUBLANES = 8
BF16_ROWS = 16
VMEM_LIMIT = 56 * 1024 * 1024
LOG2E = math.log2(math.e)

HEAD_PERM = (0, 4, 1, 5, 2, 6, 3, 7, 8, 12, 9, 13, 10, 14, 11, 15)

X0, X1 = 0, XBC_WIDTH
DT0, DT1 = X1, X1 + LANES
Q0, Q1 = DT1, DT1 + ATTN_WIDTH
K0, K1 = Q1, Q1 + KV_WIDTH
Z0, Z1 = K1, K1 + SSD_WIDTH
V0, V1 = Z1, Z1 + KV_WIDTH
G0, G1 = V1, V1 + ATTN_WIDTH


def _dot(a, b):
    return jnp.dot(a, b, preferred_element_type=F32)


def _dot_nt(a, b):
    return lax.dot_general(a, b, (((1,), (1,)), ((), ())), preferred_element_type=F32)


def _dot_tn(a, b):
    return lax.dot_general(a, b, (((0,), (0,)), ((), ())), preferred_element_type=F32)


def _const_spec(shape, single_buffer=False):
    nd = len(shape)
    kw = {"pipeline_mode": pl.Buffered(1)} if single_buffer else {}
    return pl.BlockSpec(shape, lambda *_: (0,) * nd, **kw)


def _silu(x):
    return x / (1.0 + jnp.exp(-x))


def _split3(a):
    hi = a.astype(BF16).astype(F32)
    r1 = a - hi
    mid = r1.astype(BF16).astype(F32)
    lo = (r1 - mid).astype(BF16).astype(F32)
    return hi, mid, lo


def _in_proj_kernel(x_ref, xp_ref, xn_ref, nw_ref, w_ref, cw_ref, cb_ref, cos_ref, sina_ref,
                    sinb_ref, dtb_ref, alog_ref, tri_ref, z_ref, xc_ref, a2_ref, d2_ref, d2t_ref, q_ref, k_ref,
                    v_ref, g_ref, *, tiles_per_seq):
    i = pl.program_id(0)
    tm = x_ref.shape[0]
    nw = nw_ref[...]

    def norm(x):
        ms = jnp.mean(x * x, axis=-1, keepdims=True)
        return x * lax.rsqrt(ms + EPS) * nw

    hx = jnp.concatenate([norm(x_ref[...]), norm(xp_ref[...]), norm(xn_ref[...])], axis=0).astype(BF16)
    h = hx[:tm]

    pre_all = _dot(hx, w_ref[:, X0:X1])
    pre = pre_all[:tm]
    first = (i % tiles_per_seq) == 0
    last = (i % tiles_per_seq) == tiles_per_seq - 1
    prow = jnp.where(first, 0.0, pre_all[tm + SUBLANES - 1:tm + SUBLANES, :])
    nrow = jnp.where(last, 0.0, pre_all[tm + SUBLANES:tm + SUBLANES + 1, :])
    rows = lax.broadcasted_iota(jnp.int32, (tm, 1), 0)
    up = jnp.where(rows == 0, prow, pltpu.roll(pre, 1, 0))
    dn = jnp.where(rows == tm - 1, nrow, pltpu.roll(pre, tm - 1, 0))
    cw = cw_ref[...]
    acc = up * cw[0:1, :] + pre * cw[1:2, :] + dn * cw[2:3, :] + cb_ref[...]
    xc_ref[...] = _silu(acc).astype(BF16)

    dqk = _dot(h, w_ref[:, DT0:K1])
    dtr = dqk[:, :LANES] + dtb_ref[...]
    dt = jnp.maximum(dtr, 0.0) + jnp.log1p(jnp.exp(-jnp.abs(dtr)))
    lane = lax.broadcasted_iota(jnp.int32, (1, LANES), 1)
    a = dt * jnp.where(lane < 2 * SSD_HEADS, -jnp.exp(alog_ref[...]), 0.0)
    ldt = jnp.log2(dt)
    nch = tm // CHUNK
    pieces = [jnp.concatenate(_split3(a[ci * CHUNK:(ci + 1) * CHUNK]), axis=0) for ci in range(nch)]
    both = _dot(tri_ref[...], jnp.concatenate(pieces, axis=1).astype(BF16))
    fwd_lane = lane < SSD_HEADS
    for ci in range(nch):
        rs = slice(ci * CHUNK, (ci + 1) * CHUNK)
        cs = slice(ci * LANES, (ci + 1) * LANES)
        acum = jnp.where(fwd_lane, both[:CHUNK, cs], both[CHUNK:, cs])
        acum2 = acum * LOG2E
        d2 = acum2 - ldt[rs]
        a2_ref[rs, :] = acum2
        d2_ref[rs, :] = d2
        d2t_ref[ci] = d2.T[:2 * SSD_HEADS, :]

    cosf = cos_ref[...]
    sina = sina_ref[...]
    sinb = sinb_ref[...]

    def rope(t):
        return (t * cosf + pltpu.roll(t, LANES - ROT_DIM // 2, 1) * sina
                + pltpu.roll(t, ROT_DIM // 2, 1) * sinb)

    qscale = HEAD_DIM ** -0.5 * LOG2E
    for j in range(ATTN_WIDTH // LANES):
        c0 = Q0 - DT0 + j * LANES
        q_ref[:, j * LANES:(j + 1) * LANES] = rope(dqk[:, c0:c0 + LANES] * qscale).astype(BF16)
    for j in range(KV_WIDTH // LANES):
        c0 = K0 - DT0 + j * LANES
        k_ref[:, j * LANES:(j + 1) * LANES] = rope(dqk[:, c0:c0 + LANES]).astype(BF16)

    zvg = _dot(h, w_ref[:, Z0:G1]).astype(BF16)
    z_ref[...] = zvg[:, :Z1 - Z0]
    v_ref[...] = zvg[:, V0 - Z0:V1 - Z0]
    g_ref[...] = zvg[:, G0 - Z0:]


def _in_proj(x2, norm_w, w_main, conv_w, conv_b, cosf, sina, sinb, dt_bias, a_log, seq, tm):
    t = x2.shape[0]
    tiles_per_seq = seq // tm
    cpt = tm // CHUNK
    rb = tm // SUBLANES
    nrb = t // SUBLANES
    row = lambda i: (i, 0)
    prev = lambda i: (jnp.maximum(i * rb - 1, 0), 0)
    nxt = lambda i: (jnp.minimum((i + 1) * rb, nrb - 1), 0)
    tab = lambda i: (i % tiles_per_seq, 0)
    ri = np.arange(CHUNK)[:, None]
    ci = np.arange(CHUNK)[None, :]
    tri2 = jnp.asarray(np.concatenate([np.tile(ci <= ri, (1, 3)), np.tile(ci >= ri, (1, 3))], axis=0), BF16)
    out_shapes = (
        jax.ShapeDtypeStruct((t, SSD_WIDTH), BF16),
        jax.ShapeDtypeStruct((t, XBC_WIDTH), BF16),
        jax.ShapeDtypeStruct((t, LANES), F32),
        jax.ShapeDtypeStruct((t, LANES), F32),
        jax.ShapeDtypeStruct((t // CHUNK, 2 * SSD_HEADS, CHUNK), F32),
        jax.ShapeDtypeStruct((t, ATTN_WIDTH), BF16),
        jax.ShapeDtypeStruct((t, KV_WIDTH), BF16),
        jax.ShapeDtypeStruct((t, KV_WIDTH), BF16),
        jax.ShapeDtypeStruct((t, ATTN_WIDTH), BF16),
    )
    out_specs = tuple(
        pl.BlockSpec((cpt,) + s.shape[1:], lambda i: (i, 0, 0)) if len(s.shape) == 3
        else pl.BlockSpec((tm, s.shape[1]), row) for s in out_shapes)
    return pl.pallas_call(
        functools.partial(_in_proj_kernel, tiles_per_seq=tiles_per_seq),
        grid=(t // tm,),
        in_specs=[
            pl.BlockSpec((tm, D_MODEL), row),
            pl.BlockSpec((SUBLANES, D_MODEL), prev),
            pl.BlockSpec((SUBLANES, D_MODEL), nxt),
            _const_spec((1, D_MODEL)),
            _const_spec(w_main.shape, single_buffer=True),
            _const_spec(conv_w.shape),
            _const_spec(conv_b.shape),
            pl.BlockSpec((tm, LANES), tab),
            pl.BlockSpec((tm, LANES), tab),
            pl.BlockSpec((tm, LANES), tab),
            _const_spec(dt_bias.shape),
            _const_spec(a_log.shape),
            _const_spec(tri2.shape),
        ],
        out_specs=out_specs,
        out_shape=out_shapes,
        compiler_params=pltpu.CompilerParams(
            dimension_semantics=("arbitrary",), vmem_limit_bytes=VMEM_LIMIT),
        name="in_proj",
    )(x2, x2, x2, norm_w, w_main, conv_w, conv_b, cosf, sina, sinb, dt_bias, a_log, tri2)


def _ssd_prologue(xc_ref, a2_ref, d2_ref, d2t_ref, e_bf, h_ref, reverse):
    xs_b = xc_ref[:, :SSD_WIDTH]
    bm_b = xc_ref[:, SSD_WIDTH:SSD_WIDTH + BC_WIDTH]
    cm_b = xc_ref[:, SSD_WIDTH + BC_WIDTH:]

    ri = lax.broadcasted_iota(jnp.int32, (CHUNK, CHUNK), 0)
    ci = lax.broadcasted_iota(jnp.int32, (CHUNK, CHUNK), 1)
    tri = (ci >= ri) if reverse else (ci <= ri)
    acum2 = a2_ref[...]
    last = 0 if reverse else CHUNK - 1
    a2_last = acum2[last:last + 1, :]
    lane0 = SSD_HEADS if reverse else 0
    lane = lax.broadcasted_iota(jnp.int32, (1, LANES), 1)
    mine = (lane >= lane0) & (lane < lane0 + SSD_HEADS)
    sc = jnp.where(mine, jnp.exp2(a2_last - d2_ref[...]), 0.0)
    l_hi, l_mid, l_lo = _split3(jnp.broadcast_to(a2_last, (BF16_ROWS, LANES)))
    exp_in = jnp.concatenate([sc, l_hi, l_mid, l_lo], axis=0).astype(BF16)
    exp_out = _dot(exp_in, e_bf)
    sc_exp = exp_out[:CHUNK]
    r0 = CHUNK
    a2_last_exp = (exp_out[r0:r0 + 1] + exp_out[r0 + BF16_ROWS:r0 + BF16_ROWS + 1]
                   + exp_out[r0 + 2 * BF16_ROWS:r0 + 2 * BF16_ROWS + 1])
    cd_exp = jnp.exp2(a2_last_exp)
    xsc = xs_b * sc_exp.astype(BF16)

    gmat = [_dot_nt(cm_b[:, g * D_STATE:(g + 1) * D_STATE], bm_b[:, g * D_STATE:(g + 1) * D_STATE])
            for g in range(SSD_GROUPS)]
    hprev = [h_ref[g] for g in range(SSD_GROUPS)]
    hprev_b = [h.astype(BF16) for h in hprev]
    return dict(xs_b=xs_b, bm_b=bm_b, cm_b=cm_b, tri=tri, acum2=acum2, d2t=d2t_ref[0],
                lane0=lane0, cd_exp=cd_exp, xsc=xsc, gmat=gmat, hprev=hprev,
                hprev_b=hprev_b)


def _ssd_pair(pro, j):
    pairs_per_group = SSD_HEADS // SSD_GROUPS // 2
    g = j // pairs_per_group
    jj = j % pairs_per_group
    low = lax.broadcasted_iota(jnp.int32, (CHUNK, LANES), 1) < SSD_HEAD_DIM
    zero_b = jnp.zeros((CHUNK, LANES), BF16)
    xs_pair = pro["xs_b"][:, j * LANES:(j + 1) * LANES]
    h_pair = pro["hprev_b"][g][:, jj * LANES:(jj + 1) * LANES]
    cg = pro["cm_b"][:, g * D_STATE:(g + 1) * D_STATE].astype(F32)
    acc_y = None
    for e in range(2):
        hd = pro["lane0"] + 2 * j + e
        col = jnp.broadcast_to(pro["acum2"][:, hd:hd + 1], (CHUNK, CHUNK))
        seg = col - pro["d2t"][hd:hd + 1, :]
        dec = jnp.exp2(jnp.where(pro["tri"], seg, -jnp.inf))
        m = (pro["gmat"][g] * dec).astype(BF16)
        cs = (cg * jnp.exp2(col)).astype(BF16)
        lhs = jnp.concatenate([m, cs], axis=1)
        if e == 0:
            rhs = jnp.concatenate([jnp.where(low, xs_pair, zero_b), jnp.where(low, h_pair, zero_b)], axis=0)
        else:
            rhs = jnp.concatenate([jnp.where(low, zero_b, xs_pair), jnp.where(low, zero_b, h_pair)], axis=0)
        r = _dot(lhs, rhs)
        acc_y = r if acc_y is None else acc_y + r
    return acc_y


def _ssd_state_update(pro, h_ref):
    gw = (SSD_HEADS // SSD_GROUPS) * SSD_HEAD_DIM
    for g in range(SSD_GROUPS):
        s_new = _dot_tn(pro["bm_b"][:, g * D_STATE:(g + 1) * D_STATE], pro["xsc"][:, g * gw:(g + 1) * gw])
        h_ref[g] = pro["hprev"][g] * pro["cd_exp"][:, g * gw:(g + 1) * gw] + s_new


def _ssd_kernel(xf_ref, xb_ref, a2f_ref, a2b_ref, d2f_ref, d2b_ref, d2tf_ref, d2tb_ref, dskip_ref,
                ef_ref, eb_ref, yf_ref, yb_ref, hf_ref, hb_ref):
    c = pl.program_id(1)

    @pl.when(c == 0)
    def _():
        hf_ref[...] = jnp.zeros_like(hf_ref)
        hb_ref[...] = jnp.zeros_like(hb_ref)

    pf = _ssd_prologue(xf_ref, a2f_ref, d2f_ref, d2tf_ref, ef_ref[...], hf_ref, reverse=False)
    pb = _ssd_prologue(xb_ref, a2b_ref, d2b_ref, d2tb_ref, eb_ref[...], hb_ref, reverse=True)
    for j in range(SSD_HEADS // 2):
        sl = slice(j * LANES, (j + 1) * LANES)
        yf = _ssd_pair(pf, j)
        yf_ref[:, sl] = (yf + dskip_ref[:, sl] * pf["xs_b"][:, sl].astype(F32)).astype(BF16)
        yb_ref[:, sl] = _ssd_pair(pb, j).astype(BF16)
    _ssd_state_update(pf, hf_ref)
    _ssd_state_update(pb, hb_ref)


def _ssd(xc, a2, d2, d2t, dskip_exp, e_f, e_b, batch, seq):
    t = xc.shape[0]
    nc = seq // CHUNK
    fw = lambda b, c: (b * nc + c, 0)
    bw = lambda b, c: (b * nc + nc - 1 - c, 0)
    fw3 = lambda b, c: (b * nc + c, 0, 0)
    bw3 = lambda b, c: (b * nc + nc - 1 - c, 0, 0)
    gw = (SSD_HEADS // SSD_GROUPS) * SSD_HEAD_DIM
    return pl.pallas_call(
        _ssd_kernel,
        grid=(batch, nc),
        in_specs=[
            pl.BlockSpec((CHUNK, XBC_WIDTH), fw),
            pl.BlockSpec((CHUNK, XBC_WIDTH), bw),
            pl.BlockSpec((CHUNK, LANES), fw),
            pl.BlockSpec((CHUNK, LANES), bw),
            pl.BlockSpec((CHUNK, LANES), fw),
            pl.BlockSpec((CHUNK, LANES), bw),
            pl.BlockSpec((1, 2 * SSD_HEADS, CHUNK), fw3),
            pl.BlockSpec((1, 2 * SSD_HEADS, CHUNK), bw3),
            _const_spec(dskip_exp.shape),
            _const_spec(e_f.shape),
            _const_spec(e_b.shape),
        ],
        out_specs=(pl.BlockSpec((CHUNK, SSD_WIDTH), fw), pl.BlockSpec((CHUNK, SSD_WIDTH), bw)),
        out_shape=(jax.ShapeDtypeStruct((t, SSD_WIDTH), BF16),
                   jax.ShapeDtypeStruct((t, SSD_WIDTH), BF16)),
        scratch_shapes=[pltpu.VMEM((SSD_GROUPS, D_STATE, gw), F32),
                        pltpu.VMEM((SSD_GROUPS, D_STATE, gw), F32)],
        compiler_params=pltpu.CompilerParams(
            dimension_semantics=("arbitrary", "arbitrary"), vmem_limit_bytes=VMEM_LIMIT),
        name="ssd",
    )(xc, xc, a2, a2, d2, d2, d2t, d2t, dskip_exp, e_f, e_b)


def _attn_kernel(sink_ref, q_ref, kp_ref, kc_ref, kn_ref, vp_ref, vc_ref, vn_ref, g_ref, nw_ref, o_ref):
    nb = pl.program_id(1)
    nblk = pl.num_programs(1)
    q = q_ref[...]
    kcat = jnp.concatenate([kp_ref[...], kc_ref[...], kn_ref[...]], axis=0)
    vcat = jnp.concatenate([vp_ref[...], vc_ref[...], vn_ref[...]], axis=0)

    r = lax.broadcasted_iota(jnp.int32, (BLOCK, BLOCK), 0)
    cc = lax.broadcasted_iota(jnp.int32, (BLOCK, BLOCK), 1)
    left_from = jnp.where(nb > 0, 0, BLOCK)
    right_to = jnp.where(nb < nblk - 1, 0, -BLOCK)
    bias_l = jnp.where(cc >= r + left_from, 0.0, -jnp.inf).astype(F32)
    bias_r = jnp.where(cc <= r + right_to, 0.0, -jnp.inf).astype(F32)

    low = lax.broadcasted_iota(jnp.int32, (BLOCK, LANES), 1) < HEAD_DIM
    low3 = lax.broadcasted_iota(jnp.int32, (3 * BLOCK, LANES), 1) < HEAD_DIM
    zq = jnp.zeros((BLOCK, LANES), BF16)
    tiles_per_pair = (N_HEADS // N_KV_HEADS)
    outs = []
    for j in range(N_KV_HEADS // 2):
        k_tile = kcat[:, j * LANES:(j + 1) * LANES]
        v_tile = vcat[:, j * LANES:(j + 1) * LANES]
        v_lo = jnp.where(low3, v_tile, jnp.zeros_like(v_tile))
        v_hi = jnp.where(low3, jnp.zeros_like(v_tile), v_tile)
        v2 = jnp.concatenate([v_lo, v_hi], axis=0)
        lhs_parts = []
        for tt in range(tiles_per_pair):
            qt = q[:, (j * tiles_per_pair + tt) * LANES:(j * tiles_per_pair + tt + 1) * LANES]
            lhs_parts.append(jnp.where(low, qt, zq))
            lhs_parts.append(jnp.where(low, zq, qt))
        s_all = _dot_nt(jnp.concatenate(lhs_parts, axis=0), k_tile)
        for tt in range(tiles_per_pair):
            ps = []
            invs = []
            for e in range(2):
                idx = 2 * tt + e
                head = HEAD_PERM[2 * (j * tiles_per_pair + tt) + e]
                sink = sink_ref[head] * LOG2E
                rows = slice(idx * BLOCK, (idx + 1) * BLOCK)
                s_l = s_all[rows, :BLOCK] + bias_l
                s_c = s_all[rows, BLOCK:2 * BLOCK]
                s_r = s_all[rows, 2 * BLOCK:] + bias_r
                m = jnp.max(jnp.maximum(jnp.maximum(s_l, s_c), s_r), axis=-1, keepdims=True)
                m = jnp.maximum(m, sink)
                p_l = jnp.exp2(s_l - m)
                p_c = jnp.exp2(s_c - m)
                p_r = jnp.exp2(s_r - m)
                den = jnp.sum(p_l + p_c + p_r, axis=-1, keepdims=True) + jnp.exp2(sink - m)
                ps += [p_l.astype(BF16), p_c.astype(BF16), p_r.astype(BF16)]
                invs.append(1.0 / den)
            o_t = _dot(jnp.concatenate(ps, axis=1), v2)
            outs.append(o_t * jnp.where(low, invs[0], invs[1]))
    o = jnp.concatenate(outs, axis=1)
    y = o * _silu(g_ref[...].astype(F32))
    ms = jnp.mean(y * y, axis=-1, keepdims=True)
    o_ref[...] = (y * lax.rsqrt(ms + EPS) * nw_ref[...]).astype(BF16)


def _attn(sink, q, k, v, g, norm_w, batch, seq):
    t = q.shape[0]
    nblk = seq // BLOCK
    cur = lambda b, n: (b * nblk + n, 0)
    prev = lambda b, n: (b * nblk + jnp.maximum(n - 1, 0), 0)
    nxt = lambda b, n: (b * nblk + jnp.minimum(n + 1, nblk - 1), 0)
    return pl.pallas_call(
        _attn_kernel,
        grid=(batch, nblk),
        in_specs=[
            pl.BlockSpec(memory_space=pltpu.SMEM),
            pl.BlockSpec((BLOCK, ATTN_WIDTH), cur),
            pl.BlockSpec((BLOCK, KV_WIDTH), prev),
            pl.BlockSpec((BLOCK, KV_WIDTH), cur),
            pl.BlockSpec((BLOCK, KV_WIDTH), nxt),
            pl.BlockSpec((BLOCK, KV_WIDTH), prev),
            pl.BlockSpec((BLOCK, KV_WIDTH), cur),
            pl.BlockSpec((BLOCK, KV_WIDTH), nxt),
            pl.BlockSpec((BLOCK, ATTN_WIDTH), cur),
            _const_spec(norm_w.shape),
        ],
        out_specs=pl.BlockSpec((BLOCK, ATTN_WIDTH), cur),
        out_shape=jax.ShapeDtypeStruct((t, ATTN_WIDTH), BF16),
        compiler_params=pltpu.CompilerParams(
            dimension_semantics=("arbitrary", "arbitrary"), vmem_limit_bytes=VMEM_LIMIT),
        name="attn",
    )(sink, q, k, k, k, v, v, v, g, norm_w)


def _out_kernel(x_ref, yf_ref, yb_ref, z_ref, ya_ref, p_ref, snw_ref, wo_ref, wg_ref, bg_ref, wp_ref,
                fnw_ref, o_ref):
    y = yf_ref[...].astype(F32) + yb_ref[...].astype(F32)
    y = y * _silu(z_ref[...].astype(F32))
    ms = jnp.mean(y * y, axis=-1, keepdims=True)
    y_ssd = (y * lax.rsqrt(ms + EPS) * snw_ref[...]).astype(BF16)
    x1 = (x_ref[...] + _dot(y_ssd, wo_ref[:SSD_WIDTH, :]) + _dot(ya_ref[...], wo_ref[SSD_WIDTH:, :]))
    gate_lin = _dot(x1.astype(BF16), wg_ref[...]) + bg_ref[...]
    gate = 1.0 / (1.0 + jnp.exp(-gate_lin))
    x2 = x1 + gate * _dot(p_ref[...].astype(BF16), wp_ref[...])
    ms2 = jnp.mean(x2 * x2, axis=-1, keepdims=True)
    o_ref[...] = x2 * lax.rsqrt(ms2 + EPS) * fnw_ref[...]


def _out(x2, yf, yb, z, ya, p2, ssd_norm_w, w_out, w_gate, b_gate, w_ple, final_norm_w, tm):
    t = x2.shape[0]
    row = lambda i: (i, 0)
    return pl.pallas_call(
        _out_kernel,
        grid=(t // tm,),
        in_specs=[
            pl.BlockSpec((tm, D_MODEL), row),
            pl.BlockSpec((tm, SSD_WIDTH), row),
            pl.BlockSpec((tm, SSD_WIDTH), row),
            pl.BlockSpec((tm, SSD_WIDTH), row),
            pl.BlockSpec((tm, ATTN_WIDTH), row),
            pl.BlockSpec((tm, PLE_DIM), row),
            _const_spec(ssd_norm_w.shape),
            _const_spec(w_out.shape, single_buffer=True),
            _const_spec(w_gate.shape, single_buffer=True),
            _const_spec(b_gate.shape),
            _const_spec(w_ple.shape, single_buffer=True),
            _const_spec(final_norm_w.shape),
        ],
        out_specs=pl.BlockSpec((tm, D_MODEL), row),
        out_shape=jax.ShapeDtypeStruct((t, D_MODEL), F32),
        compiler_params=pltpu.CompilerParams(
            dimension_semantics=("arbitrary",), vmem_limit_bytes=VMEM_LIMIT),
        name="out_stage",
    )(x2, yf, yb, z, ya, p2, ssd_norm_w, w_out, w_gate, b_gate, w_ple, final_norm_w)


def _rope_tables(seq):
    inv_freq = ROPE_THETA ** (-jnp.arange(0, ROT_DIM, 2, dtype=F32) / ROT_DIM)
    ang = jnp.arange(seq, dtype=F32)[:, None] * inv_freq[None, :]
    cos, sin = jnp.cos(ang), jnp.sin(ang)
    half = ROT_DIM // 2
    lane = np.arange(LANES) % HEAD_DIM
    idx = np.where(lane < half, lane, np.where(lane < ROT_DIM, lane - half, 0))
    first = jnp.asarray(lane < half)
    second = jnp.asarray((lane >= half) & (lane < ROT_DIM))
    cos_l = cos[:, idx]
    sin_l = sin[:, idx]
    cosf = jnp.where(first | second, cos_l, 1.0)
    sina = jnp.where(first, -sin_l, 0.0)
    sinb = jnp.where(second, sin_l, 0.0)
    return cosf, sina, sinb


def _perm_heads(w, axis):
    parts = [lax.slice_in_dim(w, h * HEAD_DIM, (h + 1) * HEAD_DIM, axis=axis) for h in HEAD_PERM]
    return jnp.concatenate(parts, axis=axis)


def kernel(x, p, norm_w, w_in, conv_w, conv_b, dt_bias_f, dt_bias_b, a_log_f, a_log_b, d_skip, ssd_norm_w,
           attn_sink, attn_norm_w, w_out, ple_proj, ple_gate_w, ple_gate_b, final_norm_w):
    batch, seq, _ = x.shape
    assert p.shape[0] == 1, "single-layer problem (DEPTH == 1)"
    t = batch * seq
    tm_in = min(512, seq)
    tm_out = min(256, seq)

    w = w_in[0]
    split = np.cumsum((SSD_WIDTH, XBC_WIDTH, 2 * SSD_HEADS, ATTN_WIDTH, KV_WIDTH, KV_WIDTH, ATTN_WIDTH))
    lane_pad = LANES - 2 * SSD_HEADS
    w_main = jnp.concatenate([
        w[:, split[0]:split[2]],
        jnp.zeros((D_MODEL, lane_pad), w.dtype),
        _perm_heads(w[:, split[2]:split[3]], 1),
        w[:, split[3]:split[4]],
        w[:, :split[0]],
        w[:, split[4]:split[5]],
        _perm_heads(w[:, split[5]:split[6]], 1)], axis=1).astype(BF16)
    dt_bias = jnp.pad(jnp.concatenate([dt_bias_f[0], dt_bias_b[0]]), (0, lane_pad))[None, :]
    a_log = jnp.pad(jnp.concatenate([a_log_f[0], a_log_b[0]]), (0, lane_pad))[None, :]

    x2 = x.reshape(t, D_MODEL)
    z, xc, a2, d2, d2t, q, k, v, g = _in_proj(
        x2, norm_w[0][None, :], w_main, conv_w[0], conv_b[0][None, :], *_rope_tables(seq),
        dt_bias, a_log, seq, tm_in)

    dskip_exp = jnp.repeat(d_skip[0], SSD_HEAD_DIM)[None, :]
    head_of_lane = np.arange(SSD_WIDTH) // SSD_HEAD_DIM
    e_np = (np.arange(LANES)[:, None] == head_of_lane[None, :]).astype(np.float32)
    e_f = jnp.asarray(e_np, BF16)
    e_b = jnp.asarray(np.roll(e_np, SSD_HEADS, axis=0), BF16)
    yf, yb = _ssd(xc, a2, d2, d2t, dskip_exp, e_f, e_b, batch, seq)

    ya = _attn(attn_sink[0], q, k, v, g, _perm_heads(attn_norm_w[0], 0)[None, :], batch, seq)

    w_out_b = jnp.concatenate([w_out[0][:SSD_WIDTH], _perm_heads(w_out[0][SSD_WIDTH:], 0)], axis=0).astype(BF16)
    out = _out(x2, yf, yb, z, ya, p[0].reshape(t, PLE_DIM), ssd_norm_w[0][None, :], w_out_b,
               ple_gate_w[0].astype(BF16), ple_gate_b[0][None, :], ple_proj[0].astype(BF16),
               final_norm_w[None, :], tm_out)
    return out.reshape(batch, seq, D_MODEL)
```

```python
import functools
import math

import numpy as np
import jax
import jax.numpy as jnp
from jax import lax
from jax.experimental import pallas as pl
from jax.experimental.pallas import tpu as pltpu

F32 = jnp.float32
BF16 = jnp.bfloat16

EPS = 1e-6
D_MODEL = 1024
SSD_HEADS = 16
SSD_HEAD_DIM = 64
SSD_WIDTH = SSD_HEADS * SSD_HEAD_DIM
SSD_GROUPS = 2
D_STATE = 128
BC_WIDTH = SSD_GROUPS * D_STATE
XBC_WIDTH = SSD_WIDTH + 2 * BC_WIDTH
D_CONV = 3
CHUNK = 128
HEAD_DIM = 64
N_HEADS = 16
N_KV_HEADS = 4
ATTN_WIDTH = N_HEADS * HEAD_DIM
KV_WIDTH = N_KV_HEADS * HEAD_DIM
ROT_DIM = HEAD_DIM // 4
ROPE_THETA = 500000.0
BLOCK = 128
PLE_DIM = 256

LANES = 128
SUBLANES = 8
BF16_ROWS = 16
VMEM_LIMIT = 56 * 1024 * 1024
LOG2E = math.log2(math.e)

X0, X1 = 0, XBC_WIDTH
DT0, DT1 = X1, X1 + LANES
Q0, Q1 = DT1, DT1 + ATTN_WIDTH
K0, K1 = Q1, Q1 + KV_WIDTH
Z0, Z1 = K1, K1 + SSD_WIDTH
V0, V1 = Z1, Z1 + KV_WIDTH
G0, G1 = V1, V1 + ATTN_WIDTH


def _dot(a, b):
    return jnp.dot(a, b, preferred_element_type=F32)


def _dot_nt(a, b):
    return lax.dot_general(a, b, (((1,), (1,)), ((), ())), preferred_element_type=F32)


def _dot_tn(a, b):
    return lax.dot_general(a, b, (((0,), (0,)), ((), ())), preferred_element_type=F32)


def _const_spec(shape, single_buffer=False):
    nd = len(shape)
    kw = {"pipeline_mode": pl.Buffered(1)} if single_buffer else {}
    return pl.BlockSpec(shape, lambda *_: (0,) * nd, **kw)


def _silu(x):
    return x / (1.0 + jnp.exp(-x))


def _split3(a):
    hi = a.astype(BF16).astype(F32)
    r1 = a - hi
    mid = r1.astype(BF16).astype(F32)
    lo = (r1 - mid).astype(BF16).astype(F32)
    return hi, mid, lo


def _in_proj_kernel(x_ref, xp_ref, xn_ref, nw_ref, w_ref, cw_ref, cb_ref, rcos_ref, rsin_ref, bcos_ref,
                    bsin_ref, dtb_ref, alog_ref, tri_ref, z_ref, xc_ref, a2_ref, d2_ref, d2t_ref, q_ref, k_ref,
                    ksw_ref, v_ref, vsw_ref, g_ref, *, tiles_per_seq):
    i = pl.program_id(0)
    tm = x_ref.shape[0]
    nw = nw_ref[...]

    def norm(x):
        ms = jnp.mean(x * x, axis=-1, keepdims=True)
        return x * lax.rsqrt(ms + EPS) * nw

    hx = jnp.concatenate([norm(x_ref[...]), norm(xp_ref[...]), norm(xn_ref[...])], axis=0).astype(BF16)
    h = hx[:tm]

    pre_all = _dot(hx, w_ref[:, X0:X1])
    pre = pre_all[:tm]
    first = (i % tiles_per_seq) == 0
    last = (i % tiles_per_seq) == tiles_per_seq - 1
    prow = jnp.where(first, 0.0, pre_all[tm + SUBLANES - 1:tm + SUBLANES, :])
    nrow = jnp.where(last, 0.0, pre_all[tm + SUBLANES:tm + SUBLANES + 1, :])
    rows = lax.broadcasted_iota(jnp.int32, (tm, 1), 0)
    up = jnp.where(rows == 0, prow, pltpu.roll(pre, 1, 0))
    dn = jnp.where(rows == tm - 1, nrow, pltpu.roll(pre, tm - 1, 0))
    cw = cw_ref[...]
    acc = up * cw[0:1, :] + pre * cw[1:2, :] + dn * cw[2:3, :] + cb_ref[...]
    xc_ref[...] = _silu(acc).astype(BF16)

    dqk = _dot(h, w_ref[:, DT0:K1])
    dtr = dqk[:, :LANES] + dtb_ref[...]
    dt = jnp.maximum(dtr, 0.0) + jnp.log1p(jnp.exp(-jnp.abs(dtr)))
    lane = lax.broadcasted_iota(jnp.int32, (1, LANES), 1)
    a = dt * jnp.where(lane < 2 * SSD_HEADS, -jnp.exp(alog_ref[...]), 0.0)
    ldt = jnp.log2(dt)
    nch = tm // CHUNK
    pieces = [jnp.concatenate(_split3(a[ci * CHUNK:(ci + 1) * CHUNK]), axis=0) for ci in range(nch)]
    both = _dot(tri_ref[...], jnp.concatenate(pieces, axis=1).astype(BF16))
    fwd_lane = lane < SSD_HEADS
    for ci in range(nch):
        rs = slice(ci * CHUNK, (ci + 1) * CHUNK)
        cs = slice(ci * LANES, (ci + 1) * LANES)
        acum = jnp.where(fwd_lane, both[:CHUNK, cs], both[CHUNK:, cs])
        acum2 = acum * LOG2E
        d2 = acum2 - ldt[rs]
        a2_ref[rs, :] = acum2
        d2_ref[rs, :] = d2
        d2t_ref[ci] = d2.T[:2 * SSD_HEADS, :]

    rcos, rsin = rcos_ref[...], rsin_ref[...]
    bcos, bsin = bcos_ref[0], bsin_ref[0]
    cosf = bcos * rcos - bsin * rsin
    sinf = bsin * rcos + bcos * rsin
    in_head = lane & (HEAD_DIM - 1)
    half = ROT_DIM // 2
    sina = jnp.where(in_head < half, -sinf, 0.0)
    sinb = jnp.where((in_head >= half) & (in_head < ROT_DIM), sinf, 0.0)

    def rope(t):
        return t * cosf + pltpu.roll(t, LANES - half, 1) * sina + pltpu.roll(t, half, 1) * sinb

    qscale = HEAD_DIM ** -0.5 * LOG2E
    for j in range(ATTN_WIDTH // LANES):
        c0 = Q0 - DT0 + j * LANES
        q_ref[:, j * LANES:(j + 1) * LANES] = rope(dqk[:, c0:c0 + LANES] * qscale).astype(BF16)
    for j in range(KV_WIDTH // LANES):
        c0 = K0 - DT0 + j * LANES
        cols = slice(j * LANES, (j + 1) * LANES)
        kr = rope(dqk[:, c0:c0 + LANES])
        k_ref[:, cols] = kr.astype(BF16)
        ksw_ref[:, cols] = pltpu.roll(kr, HEAD_DIM, 1).astype(BF16)

    zvg = _dot(h, w_ref[:, Z0:G1])
    z_ref[...] = zvg[:, :Z1 - Z0].astype(BF16)
    g_ref[...] = zvg[:, G0 - Z0:].astype(BF16)
    for j in range(KV_WIDTH // LANES):
        c0 = V0 - Z0 + j * LANES
        cols = slice(j * LANES, (j + 1) * LANES)
        v_ref[:, cols] = zvg[:, c0:c0 + LANES].astype(BF16)
        vsw_ref[:, cols] = pltpu.roll(zvg[:, c0:c0 + LANES], HEAD_DIM, 1).astype(BF16)


def _in_proj(x2, norm_w, w_main, conv_w, conv_b, dt_bias, a_log, seq, tm):
    t = x2.shape[0]
    tiles_per_seq = seq // tm
    cpt = tm // CHUNK
    rb = tm // SUBLANES
    nrb = t // SUBLANES
    row = lambda i: (i, 0)
    prev = lambda i: (jnp.maximum(i * rb - 1, 0), 0)
    nxt = lambda i: (jnp.minimum((i + 1) * rb, nrb - 1), 0)
    tab = lambda i: (i % tiles_per_seq, 0, 0)
    lane = np.arange(LANES) % HEAD_DIM
    pair = np.where(lane < ROT_DIM, lane % (ROT_DIM // 2), 0)
    inv_freq = ROPE_THETA ** (-jnp.arange(0, ROT_DIM, 2, dtype=F32) / ROT_DIM)
    freq = jnp.where(jnp.asarray(lane < ROT_DIM), inv_freq[pair], 0.0)[None, :]
    rang = jnp.arange(tm, dtype=F32)[:, None] * freq
    bang = (jnp.arange(tiles_per_seq, dtype=F32) * tm)[:, None, None] * freq[None]
    rcos, rsin, bcos, bsin = jnp.cos(rang), jnp.sin(rang), jnp.cos(bang), jnp.sin(bang)
    ri = np.arange(CHUNK)[:, None]
    ci = np.arange(CHUNK)[None, :]
    tri2 = jnp.asarray(np.concatenate([np.tile(ci <= ri, (1, 3)), np.tile(ci >= ri, (1, 3))], axis=0), BF16)
    out_shapes = (
        jax.ShapeDtypeStruct((t, SSD_WIDTH), BF16),
        jax.ShapeDtypeStruct((t, XBC_WIDTH), BF16),
        jax.ShapeDtypeStruct((t, LANES), F32),
        jax.ShapeDtypeStruct((t, LANES), F32),
        jax.ShapeDtypeStruct((t // CHUNK, 2 * SSD_HEADS, CHUNK), F32),
        jax.ShapeDtypeStruct((t, ATTN_WIDTH), BF16),
        jax.ShapeDtypeStruct((t, KV_WIDTH), BF16),
        jax.ShapeDtypeStruct((t, KV_WIDTH), BF16),
        jax.ShapeDtypeStruct((t, KV_WIDTH), BF16),
        jax.ShapeDtypeStruct((t, KV_WIDTH), BF16),
        jax.ShapeDtypeStruct((t, ATTN_WIDTH), BF16),
    )
    out_specs = tuple(
        pl.BlockSpec((cpt,) + s.shape[1:], lambda i: (i, 0, 0)) if len(s.shape) == 3
        else pl.BlockSpec((tm, s.shape[1]), row) for s in out_shapes)
    return pl.pallas_call(
        functools.partial(_in_proj_kernel, tiles_per_seq=tiles_per_seq),
        grid=(t // tm,),
        in_specs=[
            pl.BlockSpec((tm, D_MODEL), row),
            pl.BlockSpec((SUBLANES, D_MODEL), prev),
            pl.BlockSpec((SUBLANES, D_MODEL), nxt),
            _const_spec((1, D_MODEL)),
            _const_spec(w_main.shape, single_buffer=True),
            _const_spec(conv_w.shape),
            _const_spec(conv_b.shape),
            _const_spec(rcos.shape),
            _const_spec(rsin.shape),
            pl.BlockSpec((1, 1, LANES), tab),
            pl.BlockSpec((1, 1, LANES), tab),
            _const_spec(dt_bias.shape),
            _const_spec(a_log.shape),
            _const_spec(tri2.shape),
        ],
        out_specs=out_specs,
        out_shape=out_shapes,
        compiler_params=pltpu.CompilerParams(
            dimension_semantics=("arbitrary",), vmem_limit_bytes=VMEM_LIMIT),
        name="in_proj",
    )(x2, x2, x2, norm_w, w_main, conv_w, conv_b, rcos, rsin, bcos, bsin, dt_bias, a_log, tri2)


def _ssd_prologue(xc_ref, a2_ref, d2_ref, d2t_ref, e_bf, h_ref, reverse):
    xs_b = xc_ref[:, :SSD_WIDTH]
    bm_b = xc_ref[:, SSD_WIDTH:SSD_WIDTH + BC_WIDTH]
    cm_b = xc_ref[:, SSD_WIDTH + BC_WIDTH:]

    ri = lax.broadcasted_iota(jnp.int32, (CHUNK, CHUNK), 0)
    ci = lax.broadcasted_iota(jnp.int32, (CHUNK, CHUNK), 1)
    tri = (ci >= ri) if reverse else (ci <= ri)
    acum2 = a2_ref[...]
    last = 0 if reverse else CHUNK - 1
    a2_last = acum2[last:last + 1, :]
    lane0 = SSD_HEADS if reverse else 0
    lane = lax.broadcasted_iota(jnp.int32, (1, LANES), 1)
    mine = (lane >= lane0) & (lane < lane0 + SSD_HEADS)
    sc = jnp.where(mine, jnp.exp2(a2_last - d2_ref[...]), 0.0)
    l_hi, l_mid, l_lo = _split3(jnp.broadcast_to(a2_last, (BF16_ROWS, LANES)))
    exp_in = jnp.concatenate([sc, l_hi, l_mid, l_lo], axis=0).astype(BF16)
    exp_out = _dot(exp_in, e_bf)
    sc_exp = exp_out[:CHUNK]
    r0 = CHUNK
    a2_last_exp = (exp_out[r0:r0 + 1] + exp_out[r0 + BF16_ROWS:r0 + BF16_ROWS + 1]
                   + exp_out[r0 + 2 * BF16_ROWS:r0 + 2 * BF16_ROWS + 1])
    cd_exp = jnp.exp2(a2_last_exp)
    xsc = xs_b * sc_exp.astype(BF16)

    gmat = [_dot_nt(cm_b[:, g * D_STATE:(g + 1) * D_STATE], bm_b[:, g * D_STATE:(g + 1) * D_STATE])
            for g in range(SSD_GROUPS)]
    hprev = [h_ref[g] for g in range(SSD_GROUPS)]
    hprev_b = [h.astype(BF16) for h in hprev]
    return dict(xs_b=xs_b, bm_b=bm_b, cm_b=cm_b, tri=tri, acum2=acum2, d2t=d2t_ref[0],
                lane0=lane0, cd_exp=cd_exp, xsc=xsc, gmat=gmat, hprev=hprev,
                hprev_b=hprev_b)


def _ssd_pair(pro, j):
    pairs_per_group = SSD_HEADS // SSD_GROUPS // 2
    g = j // pairs_per_group
    jj = j % pairs_per_group
    low = lax.broadcasted_iota(jnp.int32, (CHUNK, LANES), 1) < SSD_HEAD_DIM
    zero_b = jnp.zeros((CHUNK, LANES), BF16)
    xs_pair = pro["xs_b"][:, j * LANES:(j + 1) * LANES]
    h_pair = pro["hprev_b"][g][:, jj * LANES:(jj + 1) * LANES]
    cg = pro["cm_b"][:, g * D_STATE:(g + 1) * D_STATE].astype(F32)
    acc_y = None
    for e in range(2):
        hd = pro["lane0"] + 2 * j + e
        col = jnp.broadcast_to(pro["acum2"][:, hd:hd + 1], (CHUNK, CHUNK))
        seg = col - pro["d2t"][hd:hd + 1, :]
        dec = jnp.exp2(jnp.where(pro["tri"], seg, -jnp.inf))
        m = (pro["gmat"][g] * dec).astype(BF16)
        cs = (cg * jnp.exp2(col)).astype(BF16)
        lhs = jnp.concatenate([m, cs], axis=1)
        if e == 0:
            rhs = jnp.concatenate([jnp.where(low, xs_pair, zero_b), jnp.where(low, h_pair, zero_b)], axis=0)
        else:
            rhs = jnp.concatenate([jnp.where(low, zero_b, xs_pair), jnp.where(low, zero_b, h_pair)], axis=0)
        r = _dot(lhs, rhs)
        acc_y = r if acc_y is None else acc_y + r
    return acc_y


def _ssd_state_update(pro, h_ref):
    gw = (SSD_HEADS // SSD_GROUPS) * SSD_HEAD_DIM
    for g in range(SSD_GROUPS):
        s_new = _dot_tn(pro["bm_b"][:, g * D_STATE:(g + 1) * D_STATE], pro["xsc"][:, g * gw:(g + 1) * gw])
        h_ref[g] = pro["hprev"][g] * pro["cd_exp"][:, g * gw:(g + 1) * gw] + s_new


def _ssd_kernel(xf_ref, xb_ref, a2f_ref, a2b_ref, d2f_ref, d2b_ref, d2tf_ref, d2tb_ref, dskip_ref,
                ef_ref, eb_ref, yf_ref, yb_ref, hf_ref, hb_ref):
    c = pl.program_id(1)

    @pl.when(c == 0)
    def _():
        hf_ref[...] = jnp.zeros_like(hf_ref)
        hb_ref[...] = jnp.zeros_like(hb_ref)

    pf = _ssd_prologue(xf_ref, a2f_ref, d2f_ref, d2tf_ref, ef_ref[...], hf_ref, reverse=False)
    pb = _ssd_prologue(xb_ref, a2b_ref, d2b_ref, d2tb_ref, eb_ref[...], hb_ref, reverse=True)
    for j in range(SSD_HEADS // 2):
        sl = slice(j * LANES, (j + 1) * LANES)
        yf = _ssd_pair(pf, j)
        yf_ref[:, sl] = (yf + dskip_ref[:, sl] * pf["xs_b"][:, sl].astype(F32)).astype(BF16)
        yb_ref[:, sl] = _ssd_pair(pb, j).astype(BF16)
    _ssd_state_update(pf, hf_ref)
    _ssd_state_update(pb, hb_ref)


def _ssd(xc, a2, d2, d2t, dskip_exp, e_f, e_b, batch, seq):
    t = xc.shape[0]
    nc = seq // CHUNK
    fw = lambda b, c: (b * nc + c, 0)
    bw = lambda b, c: (b * nc + nc - 1 - c, 0)
    fw3 = lambda b, c: (b * nc + c, 0, 0)
    bw3 = lambda b, c: (b * nc + nc - 1 - c, 0, 0)
    gw = (SSD_HEADS // SSD_GROUPS) * SSD_HEAD_DIM
    return pl.pallas_call(
        _ssd_kernel,
        grid=(batch, nc),
        in_specs=[
            pl.BlockSpec((CHUNK, XBC_WIDTH), fw),
            pl.BlockSpec((CHUNK, XBC_WIDTH), bw),
            pl.BlockSpec((CHUNK, LANES), fw),
            pl.BlockSpec((CHUNK, LANES), bw),
            pl.BlockSpec((CHUNK, LANES), fw),
            pl.BlockSpec((CHUNK, LANES), bw),
            pl.BlockSpec((1, 2 * SSD_HEADS, CHUNK), fw3),
            pl.BlockSpec((1, 2 * SSD_HEADS, CHUNK), bw3),
            _const_spec(dskip_exp.shape),
            _const_spec(e_f.shape),
            _const_spec(e_b.shape),
        ],
        out_specs=(pl.BlockSpec((CHUNK, SSD_WIDTH), fw), pl.BlockSpec((CHUNK, SSD_WIDTH), bw)),
        out_shape=(jax.ShapeDtypeStruct((t, SSD_WIDTH), BF16),
                   jax.ShapeDtypeStruct((t, SSD_WIDTH), BF16)),
        scratch_shapes=[pltpu.VMEM((SSD_GROUPS, D_STATE, gw), F32),
                        pltpu.VMEM((SSD_GROUPS, D_STATE, gw), F32)],
        compiler_params=pltpu.CompilerParams(
            dimension_semantics=("arbitrary", "arbitrary"), vmem_limit_bytes=VMEM_LIMIT),
        name="ssd",
    )(xc, xc, a2, a2, d2, d2, d2t, d2t, dskip_exp, e_f, e_b)


def _attn_kernel(sink_ref, q_ref, kp_ref, kc_ref, kn_ref, ksp_ref, ksc_ref, ksn_ref, vp_ref, vc_ref, vn_ref,
                 vsp_ref, vsc_ref, vsn_ref, g_ref, nw_ref, o_ref):
    nb = pl.program_id(1)
    nblk = pl.num_programs(1)
    q = q_ref[...]
    kcat = jnp.concatenate([kp_ref[...], kc_ref[...], kn_ref[...]], axis=0)
    kscat = jnp.concatenate([ksp_ref[...], ksc_ref[...], ksn_ref[...]], axis=0)
    vcat = jnp.concatenate([vp_ref[...], vc_ref[...], vn_ref[...]], axis=0)
    vscat = jnp.concatenate([vsp_ref[...], vsc_ref[...], vsn_ref[...]], axis=0)

    r = lax.broadcasted_iota(jnp.int32, (BLOCK, BLOCK), 0)
    cc = lax.broadcasted_iota(jnp.int32, (BLOCK, BLOCK), 1)
    left_from = jnp.where(nb > 0, 0, BLOCK)
    right_to = jnp.where(nb < nblk - 1, 0, -BLOCK)
    bias_l = jnp.where(cc >= r + left_from, 0.0, -jnp.inf).astype(F32)
    bias_r = jnp.where(cc <= r + right_to, 0.0, -jnp.inf).astype(F32)

    low = lax.broadcasted_iota(jnp.int32, (BLOCK, LANES), 1) < HEAD_DIM
    low3 = lax.broadcasted_iota(jnp.int32, (3 * BLOCK, LANES), 1) < HEAD_DIM
    zq = jnp.zeros((BLOCK, LANES), BF16)
    z3 = jnp.zeros((3 * BLOCK, LANES), BF16)
    tiles_per_pair = N_HEADS // N_KV_HEADS
    outs = []
    for j in range(N_KV_HEADS // 2):
        cols = slice(j * LANES, (j + 1) * LANES)
        k_t, k_s, v_t, v_s = kcat[:, cols], kscat[:, cols], vcat[:, cols], vscat[:, cols]
        to_plain, to_swapped = [], []
        for tt in range(tiles_per_pair):
            tile = j * tiles_per_pair + tt
            qt = q[:, tile * LANES:(tile + 1) * LANES]
            q_lo, q_hi = jnp.where(low, qt, zq), jnp.where(low, zq, qt)
            kv_odd = (tile // 2) % 2 == 1
            to_plain.append(q_hi if kv_odd else q_lo)
            to_swapped.append(q_lo if kv_odd else q_hi)
        s_plain = _dot_nt(jnp.concatenate(to_plain, axis=0), k_t)
        s_swapped = _dot_nt(jnp.concatenate(to_swapped, axis=0), k_s)
        v_even = jnp.concatenate([jnp.where(low3, v_t, z3), jnp.where(low3, z3, v_s)], axis=0)
        v_odd = jnp.concatenate([jnp.where(low3, v_s, z3), jnp.where(low3, z3, v_t)], axis=0)
        for tt in range(tiles_per_pair):
            tile = j * tiles_per_pair + tt
            kv_odd = (tile // 2) % 2 == 1
            rows = slice(tt * BLOCK, (tt + 1) * BLOCK)
            ps = []
            invs = []
            for e in range(2):
                s_all = s_swapped if (e == 1) != kv_odd else s_plain
                sink = sink_ref[2 * tile + e] * LOG2E
                s_l = s_all[rows, :BLOCK] + bias_l
                s_c = s_all[rows, BLOCK:2 * BLOCK]
                s_r = s_all[rows, 2 * BLOCK:] + bias_r
                m = jnp.max(jnp.maximum(jnp.maximum(s_l, s_c), s_r), axis=-1, keepdims=True)
                m = jnp.maximum(m, sink)
                p_l = jnp.exp2(s_l - m)
                p_c = jnp.exp2(s_c - m)
                p_r = jnp.exp2(s_r - m)
                den = jnp.sum(p_l + p_c + p_r, axis=-1, keepdims=True) + jnp.exp2(sink - m)
                ps += [p_l.astype(BF16), p_c.astype(BF16), p_r.astype(BF16)]
                invs.append(1.0 / den)
            o_t = _dot(jnp.concatenate(ps, axis=1), v_odd if kv_odd else v_even)
            outs.append(o_t * jnp.where(low, invs[0], invs[1]))
    o = jnp.concatenate(outs, axis=1)
    y = o * _silu(g_ref[...].astype(F32))
    ms = jnp.mean(y * y, axis=-1, keepdims=True)
    o_ref[...] = (y * lax.rsqrt(ms + EPS) * nw_ref[...]).astype(BF16)


def _attn(sink, q, k, ksw, v, vsw, g, norm_w, batch, seq):
    t = q.shape[0]
    nblk = seq // BLOCK
    cur = lambda b, n: (b * nblk + n, 0)
    prev = lambda b, n: (b * nblk + jnp.maximum(n - 1, 0), 0)
    nxt = lambda b, n: (b * nblk + jnp.minimum(n + 1, nblk - 1), 0)
    return pl.pallas_call(
        _attn_kernel,
        grid=(batch, nblk),
        in_specs=[
            pl.BlockSpec(memory_space=pltpu.SMEM),
            pl.BlockSpec((BLOCK, ATTN_WIDTH), cur),
            *([pl.BlockSpec((BLOCK, KV_WIDTH), prev), pl.BlockSpec((BLOCK, KV_WIDTH), cur),
               pl.BlockSpec((BLOCK, KV_WIDTH), nxt)] * 4),
            pl.BlockSpec((BLOCK, ATTN_WIDTH), cur),
            _const_spec(norm_w.shape),
        ],
        out_specs=pl.BlockSpec((BLOCK, ATTN_WIDTH), cur),
        out_shape=jax.ShapeDtypeStruct((t, ATTN_WIDTH), BF16),
        compiler_params=pltpu.CompilerParams(
            dimension_semantics=("arbitrary", "arbitrary"), vmem_limit_bytes=VMEM_LIMIT),
        name="attn",
    )(sink, q, k, k, k, ksw, ksw, ksw, v, v, v, vsw, vsw, vsw, g, norm_w)


def _out_kernel(x_ref, yf_ref, yb_ref, z_ref, ya_ref, p_ref, snw_ref, wo_ref, wg_ref, bg_ref, wp_ref,
                fnw_ref, o_ref):
    y = yf_ref[...].astype(F32) + yb_ref[...].astype(F32)
    y = y * _silu(z_ref[...].astype(F32))
    ms = jnp.mean(y * y, axis=-1, keepdims=True)
    y_ssd = (y * lax.rsqrt(ms + EPS) * snw_ref[...]).astype(BF16)
    x1 = (x_ref[...] + _dot(y_ssd, wo_ref[:SSD_WIDTH, :]) + _dot(ya_ref[...], wo_ref[SSD_WIDTH:, :]))
    gate_lin = _dot(x1.astype(BF16), wg_ref[...]) + bg_ref[...]
    gate = 1.0 / (1.0 + jnp.exp(-gate_lin))
    x2 = x1 + gate * _dot(p_ref[...].astype(BF16), wp_ref[...])
    ms2 = jnp.mean(x2 * x2, axis=-1, keepdims=True)
    o_ref[...] = x2 * lax.rsqrt(ms2 + EPS) * fnw_ref[...]


def _out(x2, yf, yb, z, ya, p2, ssd_norm_w, w_out, w_gate, b_gate, w_ple, final_norm_w, tm):
    t = x2.shape[0]
    row = lambda i: (i, 0)
    return pl.pallas_call(
        _out_kernel,
        grid=(t // tm,),
        in_specs=[
            pl.BlockSpec((tm, D_MODEL), row),
            pl.BlockSpec((tm, SSD_WIDTH), row),
            pl.BlockSpec((tm, SSD_WIDTH), row),
            pl.BlockSpec((tm, SSD_WIDTH), row),
            pl.BlockSpec((tm, ATTN_WIDTH), row),
            pl.BlockSpec((tm, PLE_DIM), row),
            _const_spec(ssd_norm_w.shape),
            _const_spec(w_out.shape, single_buffer=True),
            _const_spec(w_gate.shape, single_buffer=True),
            _const_spec(b_gate.shape),
            _const_spec(w_ple.shape, single_buffer=True),
            _const_spec(final_norm_w.shape),
        ],
        out_specs=pl.BlockSpec((tm, D_MODEL), row),
        out_shape=jax.ShapeDtypeStruct((t, D_MODEL), F32),
        compiler_params=pltpu.CompilerParams(
            dimension_semantics=("arbitrary",), vmem_limit_bytes=VMEM_LIMIT),
        name="out_stage",
    )(x2, yf, yb, z, ya, p2, ssd_norm_w, w_out, w_gate, b_gate, w_ple, final_norm_w)


def kernel(x, p, norm_w, w_in, conv_w, conv_b, dt_bias_f, dt_bias_b, a_log_f, a_log_b, d_skip, ssd_norm_w,
           attn_sink, attn_norm_w, w_out, ple_proj, ple_gate_w, ple_gate_b, final_norm_w):
    batch, seq, _ = x.shape
    assert p.shape[0] == 1, "single-layer problem (DEPTH == 1)"
    t = batch * seq
    tm_in = min(512, seq)
    tm_out = min(256, seq)

    w = w_in[0]
    split = np.cumsum((SSD_WIDTH, XBC_WIDTH, 2 * SSD_HEADS, ATTN_WIDTH, KV_WIDTH, KV_WIDTH, ATTN_WIDTH))
    lane_pad = LANES - 2 * SSD_HEADS
    w_main = jnp.concatenate([
        w[:, split[0]:split[2]],
        jnp.zeros((D_MODEL, lane_pad), w.dtype),
        w[:, split[2]:split[4]],
        w[:, :split[0]],
        w[:, split[4]:]], axis=1).astype(BF16)
    dt_bias = jnp.pad(jnp.concatenate([dt_bias_f[0], dt_bias_b[0]]), (0, lane_pad))[None, :]
    a_log = jnp.pad(jnp.concatenate([a_log_f[0], a_log_b[0]]), (0, lane_pad))[None, :]

    x2 = x.reshape(t, D_MODEL)
    z, xc, a2, d2, d2t, q, k, ksw, v, vsw, g = _in_proj(
        x2, norm_w[0][None, :], w_main, conv_w[0], conv_b[0][None, :], dt_bias, a_log, seq, tm_in)

    dskip_exp = jnp.repeat(d_skip[0], SSD_HEAD_DIM)[None, :]
    head_of_lane = np.arange(SSD_WIDTH) // SSD_HEAD_DIM
    e_np = (np.arange(LANES)[:, None] == head_of_lane[None, :]).astype(np.float32)
    e_f = jnp.asarray(e_np, BF16)
    e_b = jnp.asarray(np.roll(e_np, SSD_HEADS, axis=0), BF16)
    yf, yb = _ssd(xc, a2, d2, d2t, dskip_exp, e_f, e_b, batch, seq)

    ya = _attn(attn_sink[0], q, k, ksw, v, vsw, g, attn_norm_w[0][None, :], batch, seq)

    out = _out(x2, yf, yb, z, ya, p[0].reshape(t, PLE_DIM), ssd_norm_w[0][None, :], w_out[0].astype(BF16),
               ple_gate_w[0].astype(BF16), ple_gate_b[0][None, :], ple_proj[0].astype(BF16),
               final_norm_w[None, :], tm_out)
    return out.reshape(batch, seq, D_MODEL)
```

```python
import functools
import math

import numpy as np
import jax
import jax.numpy as jnp
from jax import lax
from jax.experimental import pallas as pl
from jax.experimental.pallas import tpu as pltpu

F32 = jnp.float32
BF16 = jnp.bfloat16

EPS = 1e-6
D_MODEL = 1024
SSD_HEADS = 16
SSD_HEAD_DIM = 64
SSD_WIDTH = SSD_HEADS * SSD_HEAD_DIM
SSD_GROUPS = 2
D_STATE = 128
BC_WIDTH = SSD_GROUPS * D_STATE
XBC_WIDTH = SSD_WIDTH + 2 * BC_WIDTH
D_CONV = 3
CHUNK = 128
HEAD_DIM = 64
N_HEADS = 16
N_KV_HEADS = 4
ATTN_WIDTH = N_HEADS * HEAD_DIM
KV_WIDTH = N_KV_HEADS * HEAD_DIM
ROT_DIM = HEAD_DIM // 4
ROPE_THETA = 500000.0
BLOCK = 128
PLE_DIM = 256

LANES = 128
SUBLANES = 8
BF16_ROWS = 16
VMEM_LIMIT = 56 * 1024 * 1024
LOG2E = math.log2(math.e)

HEAD_PERM = (0, 4, 1, 5, 2, 6, 3, 7, 8, 12, 9, 13, 10, 14, 11, 15)

X0, X1 = 0, XBC_WIDTH
DT0, DT1 = X1, X1 + LANES
Q0, Q1 = DT1, DT1 + ATTN_WIDTH
K0, K1 = Q1, Q1 + KV_WIDTH
Z0, Z1 = K1, K1 + SSD_WIDTH
V0, V1 = Z1, Z1 + KV_WIDTH
G0, G1 = V1, V1 + ATTN_WIDTH


def _dot(a, b):
    return jnp.dot(a, b, preferred_element_type=F32)


def _dot_nt(a, b):
    return lax.dot_general(a, b, (((1,), (1,)), ((), ())), preferred_element_type=F32)


def _dot_tn(a, b):
    return lax.dot_general(a, b, (((0,), (0,)), ((), ())), preferred_element_type=F32)


def _const_spec(shape, single_buffer=False):
    nd = len(shape)
    kw = {"pipeline_mode": pl.Buffered(1)} if single_buffer else {}
    return pl.BlockSpec(shape, lambda *_: (0,) * nd, **kw)


def _silu(x):
    return x / (1.0 + jnp.exp(-x))


def _split3(a):
    hi = a.astype(BF16).astype(F32)
    r1 = a - hi
    mid = r1.astype(BF16).astype(F32)
    lo = (r1 - mid).astype(BF16).astype(F32)
    return hi, mid, lo


W_IN_SPLIT = tuple(int(v) for v in np.cumsum(
    (SSD_WIDTH, XBC_WIDTH, 2 * SSD_HEADS, ATTN_WIDTH, KV_WIDTH, KV_WIDTH, ATTN_WIDTH)))
PREP_STEPS = 8


def _weight_prep_kernel(w_ref, wos_ref, woa_ref, wg_ref, wp_ref, wm_ref, wosb_ref, woab_ref, wgb_ref, wpb_ref):
    i = pl.program_id(0)
    end_z, end_x, end_dt, end_q, end_k, end_v, end_g = W_IN_SPLIT
    rows = w_ref.shape[0]
    lane = lax.broadcasted_iota(jnp.int32, (rows, LANES), 1)
    low = lane < HEAD_DIM

    def perm_heads(src, dst0):
        for t0 in range(0, N_HEADS // 2, 2):
            ta, tc = HEAD_PERM[2 * t0] // 2, HEAD_PERM[2 * t0 + 1] // 2
            ab = src[:, ta * LANES:(ta + 1) * LANES]
            cd = src[:, tc * LANES:(tc + 1) * LANES]
            wm_ref[:, dst0 + t0 * LANES:dst0 + (t0 + 1) * LANES] = (
                jnp.where(low, ab, pltpu.roll(cd, HEAD_DIM, 1)).astype(BF16))
            wm_ref[:, dst0 + (t0 + 1) * LANES:dst0 + (t0 + 2) * LANES] = (
                jnp.where(low, pltpu.roll(ab, HEAD_DIM, 1), cd).astype(BF16))

    wm_ref[:, X0:X1] = w_ref[:, end_z:end_x].astype(BF16)
    wm_ref[:, DT0:DT1] = jnp.where(lane < 2 * SSD_HEADS, w_ref[:, end_x:end_x + LANES], 0.0).astype(BF16)
    qk = w_ref[:, end_dt:end_k]
    perm_heads(qk, Q0)
    wm_ref[:, K0:K1] = qk[:, ATTN_WIDTH:].astype(BF16)
    wm_ref[:, Z0:Z1] = w_ref[:, :end_z].astype(BF16)
    vg = w_ref[:, end_k:end_g]
    wm_ref[:, V0:V1] = vg[:, :KV_WIDTH].astype(BF16)
    perm_heads(vg[:, KV_WIDTH:], G0)

    wosb_ref[...] = wos_ref[...].astype(BF16)
    per_kv = N_HEADS // N_KV_HEADS
    h = i + per_kv * (i // per_kv)
    first = pl.multiple_of(h * HEAD_DIM, HEAD_DIM)
    second = pl.multiple_of((h + per_kv) * HEAD_DIM, HEAD_DIM)
    woab_ref[:HEAD_DIM, :] = woa_ref[pl.ds(first, HEAD_DIM), :].astype(BF16)
    woab_ref[HEAD_DIM:, :] = woa_ref[pl.ds(second, HEAD_DIM), :].astype(BF16)
    wgb_ref[...] = wg_ref[...].astype(BF16)
    wpb_ref[...] = wp_ref[...].astype(BF16)


def _weight_prep(w_in, w_out, w_gate, w_ple):
    per_kv = N_HEADS // N_KV_HEADS
    assert PREP_STEPS * 2 == N_HEADS and all(
        HEAD_PERM[2 * i] == i + per_kv * (i // per_kv) and HEAD_PERM[2 * i + 1] == HEAD_PERM[2 * i] + per_kv
        for i in range(PREP_STEPS))
    row = lambda i: (i, 0)
    blk = lambda rows, cols: pl.BlockSpec((rows // PREP_STEPS, cols), row)
    outs = (jax.ShapeDtypeStruct((D_MODEL, G1), BF16),
            jax.ShapeDtypeStruct((SSD_WIDTH, D_MODEL), BF16),
            jax.ShapeDtypeStruct((ATTN_WIDTH, D_MODEL), BF16),
            jax.ShapeDtypeStruct(w_gate.shape, BF16),
            jax.ShapeDtypeStruct(w_ple.shape, BF16))
    return pl.pallas_call(
        _weight_prep_kernel,
        grid=(PREP_STEPS,),
        in_specs=[blk(*w_in.shape),
                  blk(SSD_WIDTH, D_MODEL),
                  pl.BlockSpec((ATTN_WIDTH, D_MODEL), lambda i: (1, 0),
                               pipeline_mode=pl.Buffered(1)),
                  blk(*w_gate.shape), blk(*w_ple.shape)],
        out_specs=tuple(blk(*s.shape) for s in outs),
        out_shape=outs,
        compiler_params=pltpu.CompilerParams(
            dimension_semantics=("arbitrary",), vmem_limit_bytes=VMEM_LIMIT),
        name="weight_prep",
    )(w_in, w_out, w_out, w_gate, w_ple)


def _in_proj_kernel(x_ref, xp_ref, xn_ref, nw_ref, w_ref, cw_ref, cb_ref, rcos_ref, rsin_ref, bcos_ref,
                    bsin_ref, dtb_ref, alog_ref, tri_ref, z_ref, xc_ref, a2_ref, d2_ref, d2t_ref, q_ref, k_ref,
                    v_ref, g_ref, *, tiles_per_seq):
    i = pl.program_id(0)
    tm = x_ref.shape[0]
    nw = nw_ref[...]

    def norm(x):
        ms = jnp.mean(x * x, axis=-1, keepdims=True)
        return x * lax.rsqrt(ms + EPS) * nw

    hx = jnp.concatenate([norm(x_ref[...]), norm(xp_ref[...]), norm(xn_ref[...])], axis=0).astype(BF16)
    h = hx[:tm]

    pre_all = _dot(hx, w_ref[:, X0:X1])
    pre = pre_all[:tm]
    first = (i % tiles_per_seq) == 0
    last = (i % tiles_per_seq) == tiles_per_seq - 1
    prow = jnp.where(first, 0.0, pre_all[tm + SUBLANES - 1:tm + SUBLANES, :])
    nrow = jnp.where(last, 0.0, pre_all[tm + SUBLANES:tm + SUBLANES + 1, :])
    rows = lax.broadcasted_iota(jnp.int32, (tm, 1), 0)
    up = jnp.where(rows == 0, prow, pltpu.roll(pre, 1, 0))
    dn = jnp.where(rows == tm - 1, nrow, pltpu.roll(pre, tm - 1, 0))
    cw = cw_ref[...]
    acc = up * cw[0:1, :] + pre * cw[1:2, :] + dn * cw[2:3, :] + cb_ref[...]
    xc_ref[...] = _silu(acc).astype(BF16)

    dqk = _dot(h, w_ref[:, DT0:K1])
    dtr = dqk[:, :LANES] + dtb_ref[...]
    dt = jnp.maximum(dtr, 0.0) + jnp.log1p(jnp.exp(-jnp.abs(dtr)))
    lane = lax.broadcasted_iota(jnp.int32, (1, LANES), 1)
    a = dt * jnp.where(lane < 2 * SSD_HEADS, -jnp.exp(alog_ref[...]), 0.0)
    ldt = jnp.log2(dt)
    nch = tm // CHUNK
    pieces = [jnp.concatenate(_split3(a[ci * CHUNK:(ci + 1) * CHUNK]), axis=0) for ci in range(nch)]
    both = _dot(tri_ref[...], jnp.concatenate(pieces, axis=1).astype(BF16))
    fwd_lane = lane < SSD_HEADS
    for ci in range(nch):
        rs = slice(ci * CHUNK, (ci + 1) * CHUNK)
        cs = slice(ci * LANES, (ci + 1) * LANES)
        acum = jnp.where(fwd_lane, both[:CHUNK, cs], both[CHUNK:, cs])
        acum2 = acum * LOG2E
        d2 = acum2 - ldt[rs]
        a2_ref[rs, :] = acum2
        d2_ref[rs, :] = d2
        d2t_ref[ci] = d2.T[:2 * SSD_HEADS, :]

    rcos, rsin = rcos_ref[...], rsin_ref[...]
    bcos, bsin = bcos_ref[0], bsin_ref[0]
    cosf = bcos * rcos - bsin * rsin
    sinf = bsin * rcos + bcos * rsin
    in_head = lane & (HEAD_DIM - 1)
    half = ROT_DIM // 2
    sina = jnp.where(in_head < half, -sinf, 0.0)
    sinb = jnp.where((in_head >= half) & (in_head < ROT_DIM), sinf, 0.0)

    def rope(t):
        return t * cosf + pltpu.roll(t, LANES - half, 1) * sina + pltpu.roll(t, half, 1) * sinb

    qscale = HEAD_DIM ** -0.5 * LOG2E
    for j in range(ATTN_WIDTH // LANES):
        c0 = Q0 - DT0 + j * LANES
        q_ref[:, j * LANES:(j + 1) * LANES] = rope(dqk[:, c0:c0 + LANES] * qscale).astype(BF16)
    for j in range(KV_WIDTH // LANES):
        c0 = K0 - DT0 + j * LANES
        k_ref[:, j * LANES:(j + 1) * LANES] = rope(dqk[:, c0:c0 + LANES]).astype(BF16)

    zvg = _dot(h, w_ref[:, Z0:G1]).astype(BF16)
    z_ref[...] = zvg[:, :Z1 - Z0]
    v_ref[...] = zvg[:, V0 - Z0:V1 - Z0]
    g_ref[...] = zvg[:, G0 - Z0:]


def _in_proj(x2, norm_w, w_main, conv_w, conv_b, dt_bias, a_log, seq, tm):
    t = x2.shape[0]
    tiles_per_seq = seq // tm
    cpt = tm // CHUNK
    rb = tm // SUBLANES
    nrb = t // SUBLANES
    row = lambda i: (i, 0)
    prev = lambda i: (jnp.maximum(i * rb - 1, 0), 0)
    nxt = lambda i: (jnp.minimum((i + 1) * rb, nrb - 1), 0)
    tab = lambda i: (i % tiles_per_seq, 0, 0)
    lane = np.arange(LANES) % HEAD_DIM
    pair = np.where(lane < ROT_DIM, lane % (ROT_DIM // 2), 0)
    inv_freq = ROPE_THETA ** (-jnp.arange(0, ROT_DIM, 2, dtype=F32) / ROT_DIM)
    freq = jnp.where(jnp.asarray(lane < ROT_DIM), inv_freq[pair], 0.0)[None, :]
    rang = jnp.arange(tm, dtype=F32)[:, None] * freq
    bang = (jnp.arange(tiles_per_seq, dtype=F32) * tm)[:, None, None] * freq[None]
    rcos, rsin, bcos, bsin = jnp.cos(rang), jnp.sin(rang), jnp.cos(bang), jnp.sin(bang)
    ri = np.arange(CHUNK)[:, None]
    ci = np.arange(CHUNK)[None, :]
    tri2 = jnp.asarray(np.concatenate([np.tile(ci <= ri, (1, 3)), np.tile(ci >= ri, (1, 3))], axis=0), BF16)
    out_shapes = (
        jax.ShapeDtypeStruct((t, SSD_WIDTH), BF16),
        jax.ShapeDtypeStruct((t, XBC_WIDTH), BF16),
        jax.ShapeDtypeStruct((t, LANES), F32),
        jax.ShapeDtypeStruct((t, LANES), F32),
        jax.ShapeDtypeStruct((t // CHUNK, 2 * SSD_HEADS, CHUNK), F32),
        jax.ShapeDtypeStruct((t, ATTN_WIDTH), BF16),
        jax.ShapeDtypeStruct((t, KV_WIDTH), BF16),
        jax.ShapeDtypeStruct((t, KV_WIDTH), BF16),
        jax.ShapeDtypeStruct((t, ATTN_WIDTH), BF16),
    )
    out_specs = tuple(
        pl.BlockSpec((cpt,) + s.shape[1:], lambda i: (i, 0, 0)) if len(s.shape) == 3
        else pl.BlockSpec((tm, s.shape[1]), row) for s in out_shapes)
    return pl.pallas_call(
        functools.partial(_in_proj_kernel, tiles_per_seq=tiles_per_seq),
        grid=(t // tm,),
        in_specs=[
            pl.BlockSpec((tm, D_MODEL), row),
            pl.BlockSpec((SUBLANES, D_MODEL), prev),
            pl.BlockSpec((SUBLANES, D_MODEL), nxt),
            _const_spec((1, D_MODEL)),
            _const_spec(w_main.shape, single_buffer=True),
            _const_spec(conv_w.shape),
            _const_spec(conv_b.shape),
            _const_spec(rcos.shape),
            _const_spec(rsin.shape),
            pl.BlockSpec((1, 1, LANES), tab),
            pl.BlockSpec((1, 1, LANES), tab),
            _const_spec(dt_bias.shape),
            _const_spec(a_log.shape),
            _const_spec(tri2.shape),
        ],
        out_specs=out_specs,
        out_shape=out_shapes,
        compiler_params=pltpu.CompilerParams(
            dimension_semantics=("arbitrary",), vmem_limit_bytes=VMEM_LIMIT),
        name="in_proj",
    )(x2, x2, x2, norm_w, w_main, conv_w, conv_b, rcos, rsin, bcos, bsin, dt_bias, a_log, tri2)


def _ssd_prologue(xc_ref, a2_ref, d2_ref, d2t_ref, e_bf, h_ref, reverse):
    xs_b = xc_ref[:, :SSD_WIDTH]
    bm_b = xc_ref[:, SSD_WIDTH:SSD_WIDTH + BC_WIDTH]
    cm_b = xc_ref[:, SSD_WIDTH + BC_WIDTH:]

    ri = lax.broadcasted_iota(jnp.int32, (CHUNK, CHUNK), 0)
    ci = lax.broadcasted_iota(jnp.int32, (CHUNK, CHUNK), 1)
    tri = (ci >= ri) if reverse else (ci <= ri)
    acum2 = a2_ref[...]
    last = 0 if reverse else CHUNK - 1
    a2_last = acum2[last:last + 1, :]
    lane0 = SSD_HEADS if reverse else 0
    lane = lax.broadcasted_iota(jnp.int32, (1, LANES), 1)
    mine = (lane >= lane0) & (lane < lane0 + SSD_HEADS)
    sc = jnp.where(mine, jnp.exp2(a2_last - d2_ref[...]), 0.0)
    l_hi, l_mid, l_lo = _split3(jnp.broadcast_to(a2_last, (BF16_ROWS, LANES)))
    exp_in = jnp.concatenate([sc, l_hi, l_mid, l_lo], axis=0).astype(BF16)
    exp_out = _dot(exp_in, e_bf)
    sc_exp = exp_out[:CHUNK]
    r0 = CHUNK
    a2_last_exp = (exp_out[r0:r0 + 1] + exp_out[r0 + BF16_ROWS:r0 + BF16_ROWS + 1]
                   + exp_out[r0 + 2 * BF16_ROWS:r0 + 2 * BF16_ROWS + 1])
    cd_exp = jnp.exp2(a2_last_exp)
    xsc = xs_b * sc_exp.astype(BF16)

    gmat = [_dot_nt(cm_b[:, g * D_STATE:(g + 1) * D_STATE], bm_b[:, g * D_STATE:(g + 1) * D_STATE])
            for g in range(SSD_GROUPS)]
    hprev = [h_ref[g] for g in range(SSD_GROUPS)]
    hprev_b = [h.astype(BF16) for h in hprev]
    return dict(xs_b=xs_b, bm_b=bm_b, cm_b=cm_b, tri=tri, acum2=acum2, d2t=d2t_ref[0],
                lane0=lane0, cd_exp=cd_exp, xsc=xsc, gmat=gmat, hprev=hprev,
                hprev_b=hprev_b)


def _ssd_pair(pro, j):
    pairs_per_group = SSD_HEADS // SSD_GROUPS // 2
    g = j // pairs_per_group
    jj = j % pairs_per_group
    low = lax.broadcasted_iota(jnp.int32, (CHUNK, LANES), 1) < SSD_HEAD_DIM
    zero_b = jnp.zeros((CHUNK, LANES), BF16)
    xs_pair = pro["xs_b"][:, j * LANES:(j + 1) * LANES]
    h_pair = pro["hprev_b"][g][:, jj * LANES:(jj + 1) * LANES]
    cg = pro["cm_b"][:, g * D_STATE:(g + 1) * D_STATE].astype(F32)
    acc_y = None
    for e in range(2):
        hd = pro["lane0"] + 2 * j + e
        col = jnp.broadcast_to(pro["acum2"][:, hd:hd + 1], (CHUNK, CHUNK))
        seg = col - pro["d2t"][hd:hd + 1, :]
        dec = jnp.exp2(jnp.where(pro["tri"], seg, -jnp.inf))
        m = (pro["gmat"][g] * dec).astype(BF16)
        cs = (cg * jnp.exp2(col)).astype(BF16)
        lhs = jnp.concatenate([m, cs], axis=1)
        if e == 0:
            rhs = jnp.concatenate([jnp.where(low, xs_pair, zero_b), jnp.where(low, h_pair, zero_b)], axis=0)
        else:
            rhs = jnp.concatenate([jnp.where(low, zero_b, xs_pair), jnp.where(low, zero_b, h_pair)], axis=0)
        r = _dot(lhs, rhs)
        acc_y = r if acc_y is None else acc_y + r
    return acc_y


def _ssd_state_update(pro, h_ref):
    gw = (SSD_HEADS // SSD_GROUPS) * SSD_HEAD_DIM
    for g in range(SSD_GROUPS):
        s_new = _dot_tn(pro["bm_b"][:, g * D_STATE:(g + 1) * D_STATE], pro["xsc"][:, g * gw:(g + 1) * gw])
        h_ref[g] = pro["hprev"][g] * pro["cd_exp"][:, g * gw:(g + 1) * gw] + s_new


def _ssd_kernel(xf_ref, xb_ref, a2f_ref, a2b_ref, d2f_ref, d2b_ref, d2tf_ref, d2tb_ref, dskip_ref,
                ef_ref, eb_ref, yf_ref, yb_ref, hf_ref, hb_ref):
    c = pl.program_id(1)

    @pl.when(c == 0)
    def _():
        hf_ref[...] = jnp.zeros_like(hf_ref)
        hb_ref[...] = jnp.zeros_like(hb_ref)

    pf = _ssd_prologue(xf_ref, a2f_ref, d2f_ref, d2tf_ref, ef_ref[...], hf_ref, reverse=False)
    pb = _ssd_prologue(xb_ref, a2b_ref, d2b_ref, d2tb_ref, eb_ref[...], hb_ref, reverse=True)
    for j in range(SSD_HEADS // 2):
        sl = slice(j * LANES, (j + 1) * LANES)
        yf = _ssd_pair(pf, j)
        yf_ref[:, sl] = (yf + dskip_ref[:, sl] * pf["xs_b"][:, sl].astype(F32)).astype(BF16)
        yb_ref[:, sl] = _ssd_pair(pb, j).astype(BF16)
    _ssd_state_update(pf, hf_ref)
    _ssd_state_update(pb, hb_ref)


def _ssd(xc, a2, d2, d2t, dskip_exp, e_f, e_b, batch, seq):
    t = xc.shape[0]
    nc = seq // CHUNK
    fw = lambda b, c: (b * nc + c, 0)
    bw = lambda b, c: (b * nc + nc - 1 - c, 0)
    fw3 = lambda b, c: (b * nc + c, 0, 0)
    bw3 = lambda b, c: (b * nc + nc - 1 - c, 0, 0)
    gw = (SSD_HEADS // SSD_GROUPS) * SSD_HEAD_DIM
    return pl.pallas_call(
        _ssd_kernel,
        grid=(batch, nc),
        in_specs=[
            pl.BlockSpec((CHUNK, XBC_WIDTH), fw),
            pl.BlockSpec((CHUNK, XBC_WIDTH), bw),
            pl.BlockSpec((CHUNK, LANES), fw),
            pl.BlockSpec((CHUNK, LANES), bw),
            pl.BlockSpec((CHUNK, LANES), fw),
            pl.BlockSpec((CHUNK, LANES), bw),
            pl.BlockSpec((1, 2 * SSD_HEADS, CHUNK), fw3),
            pl.BlockSpec((1, 2 * SSD_HEADS, CHUNK), bw3),
            _const_spec(dskip_exp.shape),
            _const_spec(e_f.shape),
            _const_spec(e_b.shape),
        ],
        out_specs=(pl.BlockSpec((CHUNK, SSD_WIDTH), fw), pl.BlockSpec((CHUNK, SSD_WIDTH), bw)),
        out_shape=(jax.ShapeDtypeStruct((t, SSD_WIDTH), BF16),
                   jax.ShapeDtypeStruct((t, SSD_WIDTH), BF16)),
        scratch_shapes=[pltpu.VMEM((SSD_GROUPS, D_STATE, gw), F32),
                        pltpu.VMEM((SSD_GROUPS, D_STATE, gw), F32)],
        compiler_params=pltpu.CompilerParams(
            dimension_semantics=("arbitrary", "arbitrary"), vmem_limit_bytes=VMEM_LIMIT),
        name="ssd",
    )(xc, xc, a2, a2, d2, d2, d2t, d2t, dskip_exp, e_f, e_b)


def _attn_kernel(sink_ref, q_ref, kp_ref, kc_ref, kn_ref, vp_ref, vc_ref, vn_ref, g_ref, nw_ref, o_ref):
    nb = pl.program_id(1)
    nblk = pl.num_programs(1)
    q = q_ref[...]
    kcat = jnp.concatenate([kp_ref[...], kc_ref[...], kn_ref[...]], axis=0)
    vcat = jnp.concatenate([vp_ref[...], vc_ref[...], vn_ref[...]], axis=0)

    r = lax.broadcasted_iota(jnp.int32, (BLOCK, BLOCK), 0)
    cc = lax.broadcasted_iota(jnp.int32, (BLOCK, BLOCK), 1)
    left_from = jnp.where(nb > 0, 0, BLOCK)
    right_to = jnp.where(nb < nblk - 1, 0, -BLOCK)
    bias_l = jnp.where(cc >= r + left_from, 0.0, -jnp.inf).astype(F32)
    bias_r = jnp.where(cc <= r + right_to, 0.0, -jnp.inf).astype(F32)

    low = lax.broadcasted_iota(jnp.int32, (BLOCK, LANES), 1) < HEAD_DIM
    low3 = lax.broadcasted_iota(jnp.int32, (3 * BLOCK, LANES), 1) < HEAD_DIM
    zq = jnp.zeros((BLOCK, LANES), BF16)
    z3 = jnp.zeros((3 * BLOCK, LANES), BF16)
    tiles_per_pair = N_HEADS // N_KV_HEADS
    outs = []
    for j in range(N_KV_HEADS // 2):
        k_tile = kcat[:, j * LANES:(j + 1) * LANES]
        v_tile = vcat[:, j * LANES:(j + 1) * LANES]
        v2 = jnp.concatenate([jnp.where(low3, v_tile, z3), jnp.where(low3, z3, v_tile)], axis=0)
        lhs_parts = []
        for tt in range(tiles_per_pair):
            tile = j * tiles_per_pair + tt
            qt = q[:, tile * LANES:(tile + 1) * LANES]
            lhs_parts += [jnp.where(low, qt, zq), jnp.where(low, zq, qt)]
        s_all = _dot_nt(jnp.concatenate(lhs_parts, axis=0), k_tile)
        for tt in range(tiles_per_pair):
            tile = j * tiles_per_pair + tt
            ps = []
            invs = []
            for e in range(2):
                sink = sink_ref[HEAD_PERM[2 * tile + e]] * LOG2E
                rows = slice((2 * tt + e) * BLOCK, (2 * tt + e + 1) * BLOCK)
                s_l = s_all[rows, :BLOCK] + bias_l
                s_c = s_all[rows, BLOCK:2 * BLOCK]
                s_r = s_all[rows, 2 * BLOCK:] + bias_r
                m = jnp.max(jnp.maximum(jnp.maximum(s_l, s_c), s_r), axis=-1, keepdims=True)
                m = jnp.maximum(m, sink)
                p_l = jnp.exp2(s_l - m)
                p_c = jnp.exp2(s_c - m)
                p_r = jnp.exp2(s_r - m)
                den = jnp.sum(p_l + p_c + p_r, axis=-1, keepdims=True) + jnp.exp2(sink - m)
                ps += [p_l.astype(BF16), p_c.astype(BF16), p_r.astype(BF16)]
                invs.append(1.0 / den)
            o_t = _dot(jnp.concatenate(ps, axis=1), v2)
            outs.append(o_t * jnp.where(low, invs[0], invs[1]))
    o = jnp.concatenate(outs, axis=1)
    y = o * _silu(g_ref[...].astype(F32))
    ms = jnp.mean(y * y, axis=-1, keepdims=True)
    nw = nw_ref[...]
    low1 = low[:1]
    nw_parts = []
    for t0 in range(0, N_HEADS // 2, 2):
        ta, tc = HEAD_PERM[2 * t0] // 2, HEAD_PERM[2 * t0 + 1] // 2
        ab, cd = nw[:, ta * LANES:(ta + 1) * LANES], nw[:, tc * LANES:(tc + 1) * LANES]
        nw_parts += [jnp.where(low1, ab, pltpu.roll(cd, HEAD_DIM, 1)),
                     jnp.where(low1, pltpu.roll(ab, HEAD_DIM, 1), cd)]
    o_ref[...] = (y * lax.rsqrt(ms + EPS) * jnp.concatenate(nw_parts, axis=1)).astype(BF16)


def _attn(sink, q, k, v, g, norm_w, batch, seq):
    t = q.shape[0]
    nblk = seq // BLOCK
    cur = lambda b, n: (b * nblk + n, 0)
    prev = lambda b, n: (b * nblk + jnp.maximum(n - 1, 0), 0)
    nxt = lambda b, n: (b * nblk + jnp.minimum(n + 1, nblk - 1), 0)
    return pl.pallas_call(
        _attn_kernel,
        grid=(batch, nblk),
        in_specs=[
            pl.BlockSpec(memory_space=pltpu.SMEM),
            pl.BlockSpec((BLOCK, ATTN_WIDTH), cur),
            *([pl.BlockSpec((BLOCK, KV_WIDTH), prev), pl.BlockSpec((BLOCK, KV_WIDTH), cur),
               pl.BlockSpec((BLOCK, KV_WIDTH), nxt)] * 2),
            pl.BlockSpec((BLOCK, ATTN_WIDTH), cur),
            _const_spec(norm_w.shape),
        ],
        out_specs=pl.BlockSpec((BLOCK, ATTN_WIDTH), cur),
        out_shape=jax.ShapeDtypeStruct((t, ATTN_WIDTH), BF16),
        compiler_params=pltpu.CompilerParams(
            dimension_semantics=("arbitrary", "arbitrary"), vmem_limit_bytes=VMEM_LIMIT),
        name="attn",
    )(sink, q, k, k, k, v, v, v, g, norm_w)


def _out_kernel(x_ref, yf_ref, yb_ref, z_ref, ya_ref, p_ref, snw_ref, wos_ref, woa_ref, wg_ref, bg_ref, wp_ref,
                fnw_ref, o_ref):
    y = yf_ref[...].astype(F32) + yb_ref[...].astype(F32)
    y = y * _silu(z_ref[...].astype(F32))
    ms = jnp.mean(y * y, axis=-1, keepdims=True)
    y_ssd = (y * lax.rsqrt(ms + EPS) * snw_ref[...]).astype(BF16)
    x1 = x_ref[...] + _dot(y_ssd, wos_ref[...]) + _dot(ya_ref[...], woa_ref[...])
    gate_lin = _dot(x1.astype(BF16), wg_ref[...]) + bg_ref[...]
    gate = 1.0 / (1.0 + jnp.exp(-gate_lin))
    x2 = x1 + gate * _dot(p_ref[...].astype(BF16), wp_ref[...])
    ms2 = jnp.mean(x2 * x2, axis=-1, keepdims=True)
    o_ref[...] = x2 * lax.rsqrt(ms2 + EPS) * fnw_ref[...]


def _out(x2, yf, yb, z, ya, p2, ssd_norm_w, w_out_ssd, w_out_attn, w_gate, b_gate, w_ple, final_norm_w, tm):
    t = x2.shape[0]
    row = lambda i: (i, 0)
    return pl.pallas_call(
        _out_kernel,
        grid=(t // tm,),
        in_specs=[
            pl.BlockSpec((tm, D_MODEL), row),
            pl.BlockSpec((tm, SSD_WIDTH), row),
            pl.BlockSpec((tm, SSD_WIDTH), row),
            pl.BlockSpec((tm, SSD_WIDTH), row),
            pl.BlockSpec((tm, ATTN_WIDTH), row),
            pl.BlockSpec((tm, PLE_DIM), row),
            _const_spec(ssd_norm_w.shape),
            _const_spec(w_out_ssd.shape, single_buffer=True),
            _const_spec(w_out_attn.shape, single_buffer=True),
            _const_spec(w_gate.shape, single_buffer=True),
            _const_spec(b_gate.shape),
            _const_spec(w_ple.shape, single_buffer=True),
            _const_spec(final_norm_w.shape),
        ],
        out_specs=pl.BlockSpec((tm, D_MODEL), row),
        out_shape=jax.ShapeDtypeStruct((t, D_MODEL), F32),
        compiler_params=pltpu.CompilerParams(
            dimension_semantics=("arbitrary",), vmem_limit_bytes=VMEM_LIMIT),
        name="out_stage",
    )(x2, yf, yb, z, ya, p2, ssd_norm_w, w_out_ssd, w_out_attn, w_gate, b_gate, w_ple, final_norm_w)


def kernel(x, p, norm_w, w_in, conv_w, conv_b, dt_bias_f, dt_bias_b, a_log_f, a_log_b, d_skip, ssd_norm_w,
           attn_sink, attn_norm_w, w_out, ple_proj, ple_gate_w, ple_gate_b, final_norm_w):
    batch, seq, _ = x.shape
    assert p.shape[0] == 1, "single-layer problem (DEPTH == 1)"
    t = batch * seq
    tm_in = min(512, seq)
    tm_out = min(256, seq)

    w_main, w_out_ssd, w_out_attn, w_gate_b, w_ple_b = _weight_prep(
        w_in[0], w_out[0], ple_gate_w[0], ple_proj[0])
    lane_pad = LANES - 2 * SSD_HEADS
    dt_bias = jnp.pad(jnp.concatenate([dt_bias_f[0], dt_bias_b[0]]), (0, lane_pad))[None, :]
    a_log = jnp.pad(jnp.concatenate([a_log_f[0], a_log_b[0]]), (0, lane_pad))[None, :]

    x2 = x.reshape(t, D_MODEL)
    z, xc, a2, d2, d2t, q, k, v, g = _in_proj(
        x2, norm_w[0][None, :], w_main, conv_w[0], conv_b[0][None, :], dt_bias, a_log, seq, tm_in)

    dskip_exp = jnp.repeat(d_skip[0], SSD_HEAD_DIM)[None, :]
    head_of_lane = np.arange(SSD_WIDTH) // SSD_HEAD_DIM
    e_np = (np.arange(LANES)[:, None] == head_of_lane[None, :]).astype(np.float32)
    e_f = jnp.asarray(e_np, BF16)
    e_b = jnp.asarray(np.roll(e_np, SSD_HEADS, axis=0), BF16)
    yf, yb = _ssd(xc, a2, d2, d2t, dskip_exp, e_f, e_b, batch, seq)

    ya = _attn(attn_sink[0], q, k, v, g, attn_norm_w[0][None, :], batch, seq)

    out = _out(x2, yf, yb, z, ya, p[0].reshape(t, PLE_DIM), ssd_norm_w[0][None, :], w_out_ssd, w_out_attn,
               w_gate_b, ple_gate_b[0][None, :], w_ple_b, final_norm_w[None, :], tm_out)
    return out.reshape(batch, seq, D_MODEL)
```

```python
import functools
import math

import numpy as np
import jax
import jax.numpy as jnp
from jax import lax
from jax.experimental import pallas as pl
from jax.experimental.pallas import tpu as pltpu

F32 = jnp.float32
BF16 = jnp.bfloat16

EPS = 1e-6
D_MODEL = 1024
SSD_HEADS = 16
SSD_HEAD_DIM = 64
SSD_WIDTH = SSD_HEADS * SSD_HEAD_DIM
SSD_GROUPS = 2
D_STATE = 128
BC_WIDTH = SSD_GROUPS * D_STATE
XBC_WIDTH = SSD_WIDTH + 2 * BC_WIDTH
D_CONV = 3
CHUNK = 128
HEAD_DIM = 64
N_HEADS = 16
N_KV_HEADS = 4
ATTN_WIDTH = N_HEADS * HEAD_DIM
KV_WIDTH = N_KV_HEADS * HEAD_DIM
ROT_DIM = HEAD_DIM // 4
ROPE_THETA = 500000.0
BLOCK = 128
PLE_DIM = 256

LANES = 128
SUBLANES = 8
BF16_ROWS = 16
VMEM_LIMIT = 56 * 1024 * 1024
LOG2E = math.log2(math.e)

HEAD_PERM = (0, 4, 1, 5, 2, 6, 3, 7, 8, 12, 9, 13, 10, 14, 11, 15)

X0, X1 = 0, XBC_WIDTH
DT0, DT1 = X1, X1 + LANES
Q0, Q1 = DT1, DT1 + ATTN_WIDTH
K0, K1 = Q1, Q1 + KV_WIDTH
Z0, Z1 = K1, K1 + SSD_WIDTH
V0, V1 = Z1, Z1 + KV_WIDTH
G0, G1 = V1, V1 + ATTN_WIDTH


def _dot(a, b):
    return jnp.dot(a, b, preferred_element_type=F32)


def _dot_nt(a, b):
    return lax.dot_general(a, b, (((1,), (1,)), ((), ())), preferred_element_type=F32)


def _dot_tn(a, b):
    return lax.dot_general(a, b, (((0,), (0,)), ((), ())), preferred_element_type=F32)


def _const_spec(shape, single_buffer=False):
    nd = len(shape)
    kw = {"pipeline_mode": pl.Buffered(1)} if single_buffer else {}
    return pl.BlockSpec(shape, lambda *_: (0,) * nd, **kw)


def _silu(x):
    return x / (1.0 + jnp.exp(-x))


def _split3(a):
    hi = a.astype(BF16).astype(F32)
    r1 = a - hi
    mid = r1.astype(BF16).astype(F32)
    lo = (r1 - mid).astype(BF16).astype(F32)
    return hi, mid, lo


W_IN_SPLIT = tuple(int(v) for v in np.cumsum(
    (SSD_WIDTH, XBC_WIDTH, 2 * SSD_HEADS, ATTN_WIDTH, KV_WIDTH, KV_WIDTH, ATTN_WIDTH)))
PREP_IN_STEPS = 4
PREP_OUT_STEPS = 8


def _prep_in_kernel(wt_ref, wm_ref):
    end_z, end_x, end_dt, end_q, end_k, end_v, end_g = W_IN_SPLIT
    cols = wt_ref.shape[1]

    def move(dst0, src0, n):
        wm_ref[dst0:dst0 + n, :] = wt_ref[src0:src0 + n, :].astype(BF16)

    move(X0, end_z, XBC_WIDTH)
    move(DT0, end_x, 2 * SSD_HEADS)
    wm_ref[DT0 + 2 * SSD_HEADS:DT1, :] = jnp.zeros((LANES - 2 * SSD_HEADS, cols), BF16)
    for i, head in enumerate(HEAD_PERM):
        move(Q0 + i * HEAD_DIM, end_dt + head * HEAD_DIM, HEAD_DIM)
        move(G0 + i * HEAD_DIM, end_v + head * HEAD_DIM, HEAD_DIM)
    move(K0, end_q, KV_WIDTH)
    move(Z0, 0, SSD_WIDTH)
    move(V0, end_k, KV_WIDTH)


def _prep_in(w_in_t):
    n, kdim = w_in_t.shape
    kb = kdim // PREP_IN_STEPS
    return pl.pallas_call(
        _prep_in_kernel,
        grid=(PREP_IN_STEPS,),
        in_specs=[pl.BlockSpec((n, kb), lambda i: (0, i))],
        out_specs=pl.BlockSpec((G1, kb), lambda i: (0, i)),
        out_shape=jax.ShapeDtypeStruct((G1, kdim), BF16),
        compiler_params=pltpu.CompilerParams(
            dimension_semantics=("arbitrary",), vmem_limit_bytes=VMEM_LIMIT),
        name="prep_in",
    )(w_in_t)


def _prep_out_kernel(wos_ref, woa_ref, wg_ref, wp_ref, wosb_ref, woab_ref, wgb_ref, wpb_ref):
    i = pl.program_id(0)
    wosb_ref[...] = wos_ref[...].astype(BF16)
    per_kv = N_HEADS // N_KV_HEADS
    h = i + per_kv * (i // per_kv)
    first = pl.multiple_of(h * HEAD_DIM, HEAD_DIM)
    second = pl.multiple_of((h + per_kv) * HEAD_DIM, HEAD_DIM)
    woab_ref[:HEAD_DIM, :] = woa_ref[pl.ds(first, HEAD_DIM), :].astype(BF16)
    woab_ref[HEAD_DIM:, :] = woa_ref[pl.ds(second, HEAD_DIM), :].astype(BF16)
    wgb_ref[...] = wg_ref[...].astype(BF16)
    wpb_ref[...] = wp_ref[...].astype(BF16)


def _prep_out(w_out, w_gate, w_ple):
    per_kv = N_HEADS // N_KV_HEADS
    steps = PREP_OUT_STEPS
    assert steps * 2 == N_HEADS and all(
        HEAD_PERM[2 * i] == i + per_kv * (i // per_kv) and HEAD_PERM[2 * i + 1] == HEAD_PERM[2 * i] + per_kv
        for i in range(steps))
    blk_in = lambda rows, cols: pl.BlockSpec((None, rows // steps, cols), lambda i: (0, i, 0))
    blk_out = lambda rows, cols: pl.BlockSpec((rows // steps, cols), lambda i: (i, 0))
    outs = (jax.ShapeDtypeStruct((SSD_WIDTH, D_MODEL), BF16),
            jax.ShapeDtypeStruct((ATTN_WIDTH, D_MODEL), BF16),
            jax.ShapeDtypeStruct(w_gate.shape[1:], BF16),
            jax.ShapeDtypeStruct(w_ple.shape[1:], BF16))
    return pl.pallas_call(
        _prep_out_kernel,
        grid=(steps,),
        in_specs=[blk_in(SSD_WIDTH, D_MODEL),
                  pl.BlockSpec((None, ATTN_WIDTH, D_MODEL), lambda i: (0, 1, 0),
                               pipeline_mode=pl.Buffered(1)),
                  blk_in(*w_gate.shape[1:]), blk_in(*w_ple.shape[1:])],
        out_specs=tuple(blk_out(*s.shape) for s in outs),
        out_shape=outs,
        compiler_params=pltpu.CompilerParams(
            dimension_semantics=("arbitrary",), vmem_limit_bytes=VMEM_LIMIT),
        name="prep_out",
    )(w_out, w_out, w_gate, w_ple)


def _in_proj_kernel(x_ref, xp_ref, xn_ref, nw_ref, w_ref, cw_ref, cb_ref, rcos_ref, rsin_ref, bcos_ref,
                    bsin_ref, dtb_ref, alog_ref, tri_ref, z_ref, xc_ref, a2_ref, d2_ref, d2t_ref, q_ref, k_ref,
                    v_ref, g_ref, *, tiles_per_seq):
    i = pl.program_id(0)
    tm = x_ref.shape[0]
    nw = nw_ref[...]

    def norm(x):
        ms = jnp.mean(x * x, axis=-1, keepdims=True)
        return x * lax.rsqrt(ms + EPS) * nw

    hx = jnp.concatenate([norm(x_ref[...]), norm(xp_ref[...]), norm(xn_ref[...])], axis=0).astype(BF16)
    h = hx[:tm]

    pre_all = _dot_nt(hx, w_ref[X0:X1, :])
    pre = pre_all[:tm]
    first = (i % tiles_per_seq) == 0
    last = (i % tiles_per_seq) == tiles_per_seq - 1
    prow = jnp.where(first, 0.0, pre_all[tm + SUBLANES - 1:tm + SUBLANES, :])
    nrow = jnp.where(last, 0.0, pre_all[tm + SUBLANES:tm + SUBLANES + 1, :])
    rows = lax.broadcasted_iota(jnp.int32, (tm, 1), 0)
    up = jnp.where(rows == 0, prow, pltpu.roll(pre, 1, 0))
    dn = jnp.where(rows == tm - 1, nrow, pltpu.roll(pre, tm - 1, 0))
    cw = cw_ref[...]
    acc = up * cw[0:1, :] + pre * cw[1:2, :] + dn * cw[2:3, :] + cb_ref[...]
    xc_ref[...] = _silu(acc).astype(BF16)

    dqk = _dot_nt(h, w_ref[DT0:K1, :])
    dtr = dqk[:, :LANES] + dtb_ref[...]
    dt = jnp.maximum(dtr, 0.0) + jnp.log1p(jnp.exp(-jnp.abs(dtr)))
    lane = lax.broadcasted_iota(jnp.int32, (1, LANES), 1)
    a = dt * jnp.where(lane < 2 * SSD_HEADS, -jnp.exp(alog_ref[...]), 0.0)
    ldt = jnp.log2(dt)
    nch = tm // CHUNK
    pieces = [jnp.concatenate(_split3(a[ci * CHUNK:(ci + 1) * CHUNK]), axis=0) for ci in range(nch)]
    both = _dot(tri_ref[...], jnp.concatenate(pieces, axis=1).astype(BF16))
    fwd_lane = lane < SSD_HEADS
    for ci in range(nch):
        rs = slice(ci * CHUNK, (ci + 1) * CHUNK)
        cs = slice(ci * LANES, (ci + 1) * LANES)
        acum = jnp.where(fwd_lane, both[:CHUNK, cs], both[CHUNK:, cs])
        acum2 = acum * LOG2E
        d2 = acum2 - ldt[rs]
        a2_ref[rs, :] = acum2
        d2_ref[rs, :] = d2
        d2t_ref[ci] = d2.T[:2 * SSD_HEADS, :]

    rcos, rsin = rcos_ref[...], rsin_ref[...]
    bcos, bsin = bcos_ref[0], bsin_ref[0]
    cosf = bcos * rcos - bsin * rsin
    sinf = bsin * rcos + bcos * rsin
    in_head = lane & (HEAD_DIM - 1)
    half = ROT_DIM // 2
    sina = jnp.where(in_head < half, -sinf, 0.0)
    sinb = jnp.where((in_head >= half) & (in_head < ROT_DIM), sinf, 0.0)

    def rope(t):
        return t * cosf + pltpu.roll(t, LANES - half, 1) * sina + pltpu.roll(t, half, 1) * sinb

    qscale = HEAD_DIM ** -0.5 * LOG2E
    for j in range(ATTN_WIDTH // LANES):
        c0 = Q0 - DT0 + j * LANES
        q_ref[:, j * LANES:(j + 1) * LANES] = rope(dqk[:, c0:c0 + LANES] * qscale).astype(BF16)
    for j in range(KV_WIDTH // LANES):
        c0 = K0 - DT0 + j * LANES
        k_ref[:, j * LANES:(j + 1) * LANES] = rope(dqk[:, c0:c0 + LANES]).astype(BF16)

    zvg = _dot_nt(h, w_ref[Z0:G1, :]).astype(BF16)
    z_ref[...] = zvg[:, :Z1 - Z0]
    v_ref[...] = zvg[:, V0 - Z0:V1 - Z0]
    g_ref[...] = zvg[:, G0 - Z0:]


def _in_proj(x2, norm_w, w_main, conv_w, conv_b, dt_bias, a_log, seq, tm):
    t = x2.shape[0]
    tiles_per_seq = seq // tm
    cpt = tm // CHUNK
    rb = tm // SUBLANES
    nrb = t // SUBLANES
    row = lambda i: (i, 0)
    prev = lambda i: (jnp.maximum(i * rb - 1, 0), 0)
    nxt = lambda i: (jnp.minimum((i + 1) * rb, nrb - 1), 0)
    tab = lambda i: (i % tiles_per_seq, 0, 0)
    lane = np.arange(LANES) % HEAD_DIM
    pair = np.where(lane < ROT_DIM, lane % (ROT_DIM // 2), 0)
    inv_freq = ROPE_THETA ** (-jnp.arange(0, ROT_DIM, 2, dtype=F32) / ROT_DIM)
    freq = jnp.where(jnp.asarray(lane < ROT_DIM), inv_freq[pair], 0.0)[None, :]
    rang = jnp.arange(tm, dtype=F32)[:, None] * freq
    bang = (jnp.arange(tiles_per_seq, dtype=F32) * tm)[:, None, None] * freq[None]
    rcos, rsin, bcos, bsin = jnp.cos(rang), jnp.sin(rang), jnp.cos(bang), jnp.sin(bang)
    ri = np.arange(CHUNK)[:, None]
    ci = np.arange(CHUNK)[None, :]
    tri2 = jnp.asarray(np.concatenate([np.tile(ci <= ri, (1, 3)), np.tile(ci >= ri, (1, 3))], axis=0), BF16)
    out_shapes = (
        jax.ShapeDtypeStruct((t, SSD_WIDTH), BF16),
        jax.ShapeDtypeStruct((t, XBC_WIDTH), BF16),
        jax.ShapeDtypeStruct((t, LANES), F32),
        jax.ShapeDtypeStruct((t, LANES), F32),
        jax.ShapeDtypeStruct((t // CHUNK, 2 * SSD_HEADS, CHUNK), F32),
        jax.ShapeDtypeStruct((t, ATTN_WIDTH), BF16),
        jax.ShapeDtypeStruct((t, KV_WIDTH), BF16),
        jax.ShapeDtypeStruct((t, KV_WIDTH), BF16),
        jax.ShapeDtypeStruct((t, ATTN_WIDTH), BF16),
    )
    out_specs = tuple(
        pl.BlockSpec((cpt,) + s.shape[1:], lambda i: (i, 0, 0)) if len(s.shape) == 3
        else pl.BlockSpec((tm, s.shape[1]), row) for s in out_shapes)
    return pl.pallas_call(
        functools.partial(_in_proj_kernel, tiles_per_seq=tiles_per_seq),
        grid=(t // tm,),
        in_specs=[
            pl.BlockSpec((tm, D_MODEL), row),
            pl.BlockSpec((SUBLANES, D_MODEL), prev),
            pl.BlockSpec((SUBLANES, D_MODEL), nxt),
            _const_spec((1, D_MODEL)),
            _const_spec(w_main.shape, single_buffer=True),
            _const_spec(conv_w.shape),
            _const_spec(conv_b.shape),
            _const_spec(rcos.shape),
            _const_spec(rsin.shape),
            pl.BlockSpec((1, 1, LANES), tab),
            pl.BlockSpec((1, 1, LANES), tab),
            _const_spec(dt_bias.shape),
            _const_spec(a_log.shape),
            _const_spec(tri2.shape),
        ],
        out_specs=out_specs,
        out_shape=out_shapes,
        compiler_params=pltpu.CompilerParams(
            dimension_semantics=("arbitrary",), vmem_limit_bytes=VMEM_LIMIT),
        name="in_proj",
    )(x2, x2, x2, norm_w, w_main, conv_w, conv_b, rcos, rsin, bcos, bsin, dt_bias, a_log, tri2)


def _ssd_prologue(xc_ref, a2_ref, d2_ref, d2t_ref, e_bf, h_ref, reverse):
    xs_b = xc_ref[:, :SSD_WIDTH]
    bm_b = xc_ref[:, SSD_WIDTH:SSD_WIDTH + BC_WIDTH]
    cm_b = xc_ref[:, SSD_WIDTH + BC_WIDTH:]

    ri = lax.broadcasted_iota(jnp.int32, (CHUNK, CHUNK), 0)
    ci = lax.broadcasted_iota(jnp.int32, (CHUNK, CHUNK), 1)
    tri = (ci >= ri) if reverse else (ci <= ri)
    acum2 = a2_ref[...]
    last = 0 if reverse else CHUNK - 1
    a2_last = acum2[last:last + 1, :]
    lane0 = SSD_HEADS if reverse else 0
    lane = lax.broadcasted_iota(jnp.int32, (1, LANES), 1)
    mine = (lane >= lane0) & (lane < lane0 + SSD_HEADS)
    sc = jnp.where(mine, jnp.exp2(a2_last - d2_ref[...]), 0.0)
    l_hi, l_mid, l_lo = _split3(jnp.broadcast_to(a2_last, (BF16_ROWS, LANES)))
    exp_in = jnp.concatenate([sc, l_hi, l_mid, l_lo], axis=0).astype(BF16)
    exp_out = _dot(exp_in, e_bf)
    sc_exp = exp_out[:CHUNK]
    r0 = CHUNK
    a2_last_exp = (exp_out[r0:r0 + 1] + exp_out[r0 + BF16_ROWS:r0 + BF16_ROWS + 1]
                   + exp_out[r0 + 2 * BF16_ROWS:r0 + 2 * BF16_ROWS + 1])
    cd_exp = jnp.exp2(a2_last_exp)
    xsc = xs_b * sc_exp.astype(BF16)

    gmat = [_dot_nt(cm_b[:, g * D_STATE:(g + 1) * D_STATE], bm_b[:, g * D_STATE:(g + 1) * D_STATE])
            for g in range(SSD_GROUPS)]
    hprev = [h_ref[g] for g in range(SSD_GROUPS)]
    hprev_b = [h.astype(BF16) for h in hprev]
    return dict(xs_b=xs_b, bm_b=bm_b, cm_b=cm_b, tri=tri, acum2=acum2, d2t=d2t_ref[0],
                lane0=lane0, cd_exp=cd_exp, xsc=xsc, gmat=gmat, hprev=hprev,
                hprev_b=hprev_b)


def _ssd_pair(pro, j):
    pairs_per_group = SSD_HEADS // SSD_GROUPS // 2
    g = j // pairs_per_group
    jj = j % pairs_per_group
    low = lax.broadcasted_iota(jnp.int32, (CHUNK, LANES), 1) < SSD_HEAD_DIM
    zero_b = jnp.zeros((CHUNK, LANES), BF16)
    xs_pair = pro["xs_b"][:, j * LANES:(j + 1) * LANES]
    h_pair = pro["hprev_b"][g][:, jj * LANES:(jj + 1) * LANES]
    cg = pro["cm_b"][:, g * D_STATE:(g + 1) * D_STATE].astype(F32)
    acc_y = None
    for e in range(2):
        hd = pro["lane0"] + 2 * j + e
        col = jnp.broadcast_to(pro["acum2"][:, hd:hd + 1], (CHUNK, CHUNK))
        seg = col - pro["d2t"][hd:hd + 1, :]
        dec = jnp.exp2(jnp.where(pro["tri"], seg, -jnp.inf))
        m = (pro["gmat"][g] * dec).astype(BF16)
        cs = (cg * jnp.exp2(col)).astype(BF16)
        lhs = jnp.concatenate([m, cs], axis=1)
        if e == 0:
            rhs = jnp.concatenate([jnp.where(low, xs_pair, zero_b), jnp.where(low, h_pair, zero_b)], axis=0)
        else:
            rhs = jnp.concatenate([jnp.where(low, zero_b, xs_pair), jnp.where(low, zero_b, h_pair)], axis=0)
        r = _dot(lhs, rhs)
        acc_y = r if acc_y is None else acc_y + r
    return acc_y


def _ssd_state_update(pro, h_ref):
    gw = (SSD_HEADS // SSD_GROUPS) * SSD_HEAD_DIM
    for g in range(SSD_GROUPS):
        s_new = _dot_tn(pro["bm_b"][:, g * D_STATE:(g + 1) * D_STATE], pro["xsc"][:, g * gw:(g + 1) * gw])
        h_ref[g] = pro["hprev"][g] * pro["cd_exp"][:, g * gw:(g + 1) * gw] + s_new


def _ssd_kernel(xf_ref, xb_ref, a2f_ref, a2b_ref, d2f_ref, d2b_ref, d2tf_ref, d2tb_ref, dskip_ref,
                ef_ref, eb_ref, yf_ref, yb_ref, hf_ref, hb_ref):
    c = pl.program_id(1)

    @pl.when(c == 0)
    def _():
        hf_ref[...] = jnp.zeros_like(hf_ref)
        hb_ref[...] = jnp.zeros_like(hb_ref)

    pf = _ssd_prologue(xf_ref, a2f_ref, d2f_ref, d2tf_ref, ef_ref[...], hf_ref, reverse=False)
    pb = _ssd_prologue(xb_ref, a2b_ref, d2b_ref, d2tb_ref, eb_ref[...], hb_ref, reverse=True)
    for j in range(SSD_HEADS // 2):
        sl = slice(j * LANES, (j + 1) * LANES)
        yf = _ssd_pair(pf, j)
        yf_ref[:, sl] = (yf + dskip_ref[:, sl] * pf["xs_b"][:, sl].astype(F32)).astype(BF16)
        yb_ref[:, sl] = _ssd_pair(pb, j).astype(BF16)
    _ssd_state_update(pf, hf_ref)
    _ssd_state_update(pb, hb_ref)


def _ssd(xc, a2, d2, d2t, dskip_exp, e_f, e_b, batch, seq):
    t = xc.shape[0]
    nc = seq // CHUNK
    fw = lambda b, c: (b * nc + c, 0)
    bw = lambda b, c: (b * nc + nc - 1 - c, 0)
    fw3 = lambda b, c: (b * nc + c, 0, 0)
    bw3 = lambda b, c: (b * nc + nc - 1 - c, 0, 0)
    gw = (SSD_HEADS // SSD_GROUPS) * SSD_HEAD_DIM
    return pl.pallas_call(
        _ssd_kernel,
        grid=(batch, nc),
        in_specs=[
            pl.BlockSpec((CHUNK, XBC_WIDTH), fw),
            pl.BlockSpec((CHUNK, XBC_WIDTH), bw),
            pl.BlockSpec((CHUNK, LANES), fw),
            pl.BlockSpec((CHUNK, LANES), bw),
            pl.BlockSpec((CHUNK, LANES), fw),
            pl.BlockSpec((CHUNK, LANES), bw),
            pl.BlockSpec((1, 2 * SSD_HEADS, CHUNK), fw3),
            pl.BlockSpec((1, 2 * SSD_HEADS, CHUNK), bw3),
            _const_spec(dskip_exp.shape),
            _const_spec(e_f.shape),
            _const_spec(e_b.shape),
        ],
        out_specs=(pl.BlockSpec((CHUNK, SSD_WIDTH), fw), pl.BlockSpec((CHUNK, SSD_WIDTH), bw)),
        out_shape=(jax.ShapeDtypeStruct((t, SSD_WIDTH), BF16),
                   jax.ShapeDtypeStruct((t, SSD_WIDTH), BF16)),
        scratch_shapes=[pltpu.VMEM((SSD_GROUPS, D_STATE, gw), F32),
                        pltpu.VMEM((SSD_GROUPS, D_STATE, gw), F32)],
        compiler_params=pltpu.CompilerParams(
            dimension_semantics=("arbitrary", "arbitrary"), vmem_limit_bytes=VMEM_LIMIT),
        name="ssd",
    )(xc, xc, a2, a2, d2, d2, d2t, d2t, dskip_exp, e_f, e_b)


def _attn_kernel(sink_ref, q_ref, kp_ref, kc_ref, kn_ref, vp_ref, vc_ref, vn_ref, g_ref, nw_ref, o_ref):
    nb = pl.program_id(1)
    nblk = pl.num_programs(1)
    q = q_ref[...]
    kcat = jnp.concatenate([kp_ref[...], kc_ref[...], kn_ref[...]], axis=0)
    vcat = jnp.concatenate([vp_ref[...], vc_ref[...], vn_ref[...]], axis=0)

    r = lax.broadcasted_iota(jnp.int32, (BLOCK, BLOCK), 0)
    cc = lax.broadcasted_iota(jnp.int32, (BLOCK, BLOCK), 1)
    left_from = jnp.where(nb > 0, 0, BLOCK)
    right_to = jnp.where(nb < nblk - 1, 0, -BLOCK)
    bias_l = jnp.where(cc >= r + left_from, 0.0, -jnp.inf).astype(F32)
    bias_r = jnp.where(cc <= r + right_to, 0.0, -jnp.inf).astype(F32)

    low = lax.broadcasted_iota(jnp.int32, (BLOCK, LANES), 1) < HEAD_DIM
    low3 = lax.broadcasted_iota(jnp.int32, (3 * BLOCK, LANES), 1) < HEAD_DIM
    zq = jnp.zeros((BLOCK, LANES), BF16)
    z3 = jnp.zeros((3 * BLOCK, LANES), BF16)
    tiles_per_pair = N_HEADS // N_KV_HEADS
    outs = []
    for j in range(N_KV_HEADS // 2):
        k_tile = kcat[:, j * LANES:(j + 1) * LANES]
        v_tile = vcat[:, j * LANES:(j + 1) * LANES]
        v2 = jnp.concatenate([jnp.where(low3, v_tile, z3), jnp.where(low3, z3, v_tile)], axis=0)
        lhs_parts = []
        for tt in range(tiles_per_pair):
            tile = j * tiles_per_pair + tt
            qt = q[:, tile * LANES:(tile + 1) * LANES]
            lhs_parts += [jnp.where(low, qt, zq), jnp.where(low, zq, qt)]
        s_all = _dot_nt(jnp.concatenate(lhs_parts, axis=0), k_tile)
        for tt in range(tiles_per_pair):
            tile = j * tiles_per_pair + tt
            ps = []
            invs = []
            for e in range(2):
                sink = sink_ref[HEAD_PERM[2 * tile + e]] * LOG2E
                rows = slice((2 * tt + e) * BLOCK, (2 * tt + e + 1) * BLOCK)
                s_l = s_all[rows, :BLOCK] + bias_l
                s_c = s_all[rows, BLOCK:2 * BLOCK]
                s_r = s_all[rows, 2 * BLOCK:] + bias_r
                m = jnp.max(jnp.maximum(jnp.maximum(s_l, s_c), s_r), axis=-1, keepdims=True)
                m = jnp.maximum(m, sink)
                p_l = jnp.exp2(s_l - m)
                p_c = jnp.exp2(s_c - m)
                p_r = jnp.exp2(s_r - m)
                den = jnp.sum(p_l + p_c + p_r, axis=-1, keepdims=True) + jnp.exp2(sink - m)
                ps += [p_l.astype(BF16), p_c.astype(BF16), p_r.astype(BF16)]
                invs.append(1.0 / den)
            o_t = _dot(jnp.concatenate(ps, axis=1), v2)
            outs.append(o_t * jnp.where(low, invs[0], invs[1]))
    o = jnp.concatenate(outs, axis=1)
    y = o * _silu(g_ref[...].astype(F32))
    ms = jnp.mean(y * y, axis=-1, keepdims=True)
    nw = nw_ref[...]
    low1 = low[:1]
    nw_parts = []
    for t0 in range(0, N_HEADS // 2, 2):
        ta, tc = HEAD_PERM[2 * t0] // 2, HEAD_PERM[2 * t0 + 1] // 2
        ab, cd = nw[:, ta * LANES:(ta + 1) * LANES], nw[:, tc * LANES:(tc + 1) * LANES]
        nw_parts += [jnp.where(low1, ab, pltpu.roll(cd, HEAD_DIM, 1)),
                     jnp.where(low1, pltpu.roll(ab, HEAD_DIM, 1), cd)]
    o_ref[...] = (y * lax.rsqrt(ms + EPS) * jnp.concatenate(nw_parts, axis=1)).astype(BF16)


def _attn(sink, q, k, v, g, norm_w, batch, seq):
    t = q.shape[0]
    nblk = seq // BLOCK
    cur = lambda b, n: (b * nblk + n, 0)
    prev = lambda b, n: (b * nblk + jnp.maximum(n - 1, 0), 0)
    nxt = lambda b, n: (b * nblk + jnp.minimum(n + 1, nblk - 1), 0)
    return pl.pallas_call(
        _attn_kernel,
        grid=(batch, nblk),
        in_specs=[
            pl.BlockSpec(memory_space=pltpu.SMEM),
            pl.BlockSpec((BLOCK, ATTN_WIDTH), cur),
            *([pl.BlockSpec((BLOCK, KV_WIDTH), prev), pl.BlockSpec((BLOCK, KV_WIDTH), cur),
               pl.BlockSpec((BLOCK, KV_WIDTH), nxt)] * 2),
            pl.BlockSpec((BLOCK, ATTN_WIDTH), cur),
            _const_spec(norm_w.shape),
        ],
        out_specs=pl.BlockSpec((BLOCK, ATTN_WIDTH), cur),
        out_shape=jax.ShapeDtypeStruct((t, ATTN_WIDTH), BF16),
        compiler_params=pltpu.CompilerParams(
            dimension_semantics=("arbitrary", "arbitrary"), vmem_limit_bytes=VMEM_LIMIT),
        name="attn",
    )(sink, q, k, k, k, v, v, v, g, norm_w)


def _out_kernel(x_ref, yf_ref, yb_ref, z_ref, ya_ref, p_ref, snw_ref, wos_ref, woa_ref, wg_ref, bg_ref, wp_ref,
                fnw_ref, o_ref):
    y = yf_ref[...].astype(F32) + yb_ref[...].astype(F32)
    y = y * _silu(z_ref[...].astype(F32))
    ms = jnp.mean(y * y, axis=-1, keepdims=True)
    y_ssd = (y * lax.rsqrt(ms + EPS) * snw_ref[...]).astype(BF16)
    x1 = x_ref[...] + _dot(y_ssd, wos_ref[...]) + _dot(ya_ref[...], woa_ref[...])
    gate_lin = _dot(x1.astype(BF16), wg_ref[...]) + bg_ref[...]
    gate = 1.0 / (1.0 + jnp.exp(-gate_lin))
    x2 = x1 + gate * _dot(p_ref[...].astype(BF16), wp_ref[...])
    ms2 = jnp.mean(x2 * x2, axis=-1, keepdims=True)
    o_ref[...] = x2 * lax.rsqrt(ms2 + EPS) * fnw_ref[...]


def _out(x2, yf, yb, z, ya, p2, ssd_norm_w, w_out_ssd, w_out_attn, w_gate, b_gate, w_ple, final_norm_w, tm):
    t = x2.shape[0]
    row = lambda i: (i, 0)
    return pl.pallas_call(
        _out_kernel,
        grid=(t // tm,),
        in_specs=[
            pl.BlockSpec((tm, D_MODEL), row),
            pl.BlockSpec((tm, SSD_WIDTH), row),
            pl.BlockSpec((tm, SSD_WIDTH), row),
            pl.BlockSpec((tm, SSD_WIDTH), row),
            pl.BlockSpec((tm, ATTN_WIDTH), row),
            pl.BlockSpec((tm, PLE_DIM), row),
            _const_spec(ssd_norm_w.shape),
            _const_spec(w_out_ssd.shape, single_buffer=True),
            _const_spec(w_out_attn.shape, single_buffer=True),
            _const_spec(w_gate.shape, single_buffer=True),
            _const_spec(b_gate.shape),
            _const_spec(w_ple.shape, single_buffer=True),
            _const_spec(final_norm_w.shape),
        ],
        out_specs=pl.BlockSpec((tm, D_MODEL), row),
        out_shape=jax.ShapeDtypeStruct((t, D_MODEL), F32),
        compiler_params=pltpu.CompilerParams(
            dimension_semantics=("arbitrary",), vmem_limit_bytes=VMEM_LIMIT),
        name="out_stage",
    )(x2, yf, yb, z, ya, p2, ssd_norm_w, w_out_ssd, w_out_attn, w_gate, b_gate, w_ple, final_norm_w)


def kernel(x, p, norm_w, w_in, conv_w, conv_b, dt_bias_f, dt_bias_b, a_log_f, a_log_b, d_skip, ssd_norm_w,
           attn_sink, attn_norm_w, w_out, ple_proj, ple_gate_w, ple_gate_b, final_norm_w):
    batch, seq, _ = x.shape
    assert p.shape[0] == 1, "single-layer problem (DEPTH == 1)"
    t = batch * seq
    tm_in = min(1024, seq)
    tm_out = min(256, seq)

    w_main = _prep_in(w_in[0].T)
    w_out_ssd, w_out_attn, w_gate_b, w_ple_b = _prep_out(w_out, ple_gate_w, ple_proj)
    lane_pad = LANES - 2 * SSD_HEADS
    dt_bias = jnp.pad(jnp.concatenate([dt_bias_f[0], dt_bias_b[0]]), (0, lane_pad))[None, :]
    a_log = jnp.pad(jnp.concatenate([a_log_f[0], a_log_b[0]]), (0, lane_pad))[None, :]

    x2 = x.reshape(t, D_MODEL)
    z, xc, a2, d2, d2t, q, k, v, g = _in_proj(
        x2, norm_w[0][None, :], w_main, conv_w[0], conv_b[0][None, :], dt_bias, a_log, seq, tm_in)

    dskip_exp = jnp.repeat(d_skip[0], SSD_HEAD_DIM)[None, :]
    head_of_lane = np.arange(SSD_WIDTH) // SSD_HEAD_DIM
    e_np = (np.arange(LANES)[:, None] == head_of_lane[None, :]).astype(np.float32)
    e_f = jnp.asarray(e_np, BF16)
    e_b = jnp.asarray(np.roll(e_np, SSD_HEADS, axis=0), BF16)
    yf, yb = _ssd(xc, a2, d2, d2t, dskip_exp, e_f, e_b, batch, seq)

    ya = _attn(attn_sink[0], q, k, v, g, attn_norm_w[0][None, :], batch, seq)

    out = _out(x2, yf, yb, z, ya, p[0].reshape(t, PLE_DIM), ssd_norm_w[0][None, :], w_out_ssd, w_out_attn,
               w_gate_b, ple_gate_b[0][None, :], w_ple_b, final_norm_w[None, :], tm_out)
    return out.reshape(batch, seq, D_MODEL)
```
